```python
import jax, jax.numpy as jnp
from jax import lax
import numpy as np

D_MODEL = 1024
BATCH = 1
SEQ = 16384
DEPTH = 1

CHUNK = 64
Q_BLOCK = 128
EPS = 1e-6
MAX_STREAM_OFFSET = 4096

D_RNN = D_MODEL
LRU_BLOCK_W = 256
LRU_BLOCKS = D_RNN // LRU_BLOCK_W
CONV_W = 4
LRU_C = 8.0

MLA_HEADS = 16
V_HEAD = D_MODEL // MLA_HEADS
QK_NOPE = 64
QK_ROPE = 32
QK_HEAD = QK_NOPE + QK_ROPE
Q_LORA = 768
KV_LORA = 256
ROPE_THETA = 10000.0

N_BRANCH = 2
IN_SPLITS = (D_RNN, D_RNN, Q_LORA, KV_LORA, QK_ROPE, N_BRANCH * D_MODEL)
D_IN = sum(IN_SPLITS)

N_GROUPS = 4
EXPERTS_PER_GROUP = 8
N_EXPERTS = N_GROUPS * EXPERTS_PER_GROUP
TOP_K = 2
D_EXPERT = 256
MOE_BLOCK = 128

kernel_name = "hybrid_rglru_mla_hiermoe_block"


def rms_norm(x, g):
    xf = x.astype(jnp.float32)
    y = xf * lax.rsqrt(jnp.mean(xf * xf, axis=-1, keepdims=True) + EPS)
    return (y * g.astype(jnp.float32)).astype(x.dtype)


def causal_conv(x, w, b):
    y = lax.conv_general_dilated(
        x, w[:, None, :].astype(x.dtype), window_strides=(1,),
        padding=[(CONV_W - 1, 0)], dimension_numbers=('NWC', 'WIO', 'NWC'),
        feature_group_count=x.shape[-1])
    return y + b.astype(x.dtype)


def rg_lru(x, wa, ba, wx, bx, lam):
    B, S, _ = x.shape
    xb = x.reshape(B, S, LRU_BLOCKS, LRU_BLOCK_W)
    r = jax.nn.sigmoid((jnp.einsum('bshi,hij->bshj', xb, wa).reshape(B, S, D_RNN) + ba).astype(jnp.float32))
    i = jax.nn.sigmoid((jnp.einsum('bshi,hij->bshj', xb, wx).reshape(B, S, D_RNN) + bx).astype(jnp.float32))
    log_a = -LRU_C * r * jax.nn.softplus(-lam.astype(jnp.float32))
    a = jnp.exp(log_a)
    mult = jnp.sqrt(-jnp.expm1(2.0 * log_a))
    b = mult * (i * x.astype(jnp.float32))

    def combine(left, right):
        a_l, b_l = left
        a_r, b_r = right
        return a_l * a_r, a_r * b_l + b_r

    _, h = lax.associative_scan(combine, (a, b), axis=1)
    return h.astype(x.dtype)


def rope(x, pos):
    half = QK_ROPE // 2
    freq = ROPE_THETA ** (-jnp.arange(half, dtype=jnp.float32) / half)
    ang = pos.astype(jnp.float32)[:, :, None, None] * freq
    cos, sin = jnp.cos(ang), jnp.sin(ang)
    x1 = x[..., :half].astype(jnp.float32)
    x2 = x[..., half:].astype(jnp.float32)
    return jnp.concatenate([x1 * cos - x2 * sin, x1 * sin + x2 * cos], axis=-1).astype(x.dtype)


def chunk_causal_attention(q, k, v):
    B, S, H, _ = q.shape
    scale = QK_HEAD ** -0.5
    outs = []
    for blk in range(S // Q_BLOCK):
        q0 = blk * Q_BLOCK
        k_end = q0 + Q_BLOCK
        qb = q[:, q0:k_end]
        s = jnp.einsum('bqhd,bkhd->bhqk', qb, k[:, :k_end]).astype(jnp.float32) * scale
        q_chunk = (q0 + jnp.arange(Q_BLOCK)) // CHUNK
        k_chunk = jnp.arange(k_end) // CHUNK
        mask = k_chunk[None, :] <= q_chunk[:, None]
        s = jnp.where(mask, s, -jnp.inf)
        p = jax.nn.softmax(s, axis=-1).astype(v.dtype)
        outs.append(jnp.einsum('bhqk,bkhd->bqhd', p, v[:, :k_end]))
    return jnp.concatenate(outs, axis=1).reshape(B, S, H * V_HEAD)


def mla(q_c, kv_c, k_pe, pos, q_a_g, w_uq, kv_a_g, w_ukv, q_norm_g, k_norm_g):
    B, S, _ = q_c.shape
    q = (rms_norm(q_c, q_a_g) @ w_uq).reshape(B, S, MLA_HEADS, QK_HEAD)
    kv = (rms_norm(kv_c, kv_a_g) @ w_ukv).reshape(B, S, MLA_HEADS, QK_NOPE + V_HEAD)
    k_nope, v = kv[..., :QK_NOPE], kv[..., QK_NOPE:]
    k_pe_h = jnp.broadcast_to(k_pe[:, :, None, :], (B, S, MLA_HEADS, QK_ROPE))
    k = jnp.concatenate([k_nope, k_pe_h], axis=-1)
    q = rms_norm(q, q_norm_g)
    k = rms_norm(k, k_norm_g)
    q = jnp.concatenate([q[..., :QK_NOPE], rope(q[..., QK_NOPE:], pos)], axis=-1)
    k = jnp.concatenate([k[..., :QK_NOPE], rope(k[..., QK_NOPE:], pos)], axis=-1)
    return chunk_causal_attention(q, k, v)


def hier_moe(x, wg, bg, we, be, w_gate, w_up, w_down):
    B, S, D = x.shape
    T = B * S
    xt = x.reshape(T, D)
    xf = xt.astype(jnp.float32)
    g_logits = xf @ wg.astype(jnp.float32) + bg.astype(jnp.float32)
    g_prob = jax.nn.softmax(g_logits, axis=-1)
    grp = jnp.argmax(g_logits, axis=-1)
    p_grp = jnp.take_along_axis(g_prob, grp[:, None], axis=1)
    e_logits = (xf @ we.astype(jnp.float32) + be.astype(jnp.float32)).reshape(T, N_GROUPS, EXPERTS_PER_GROUP)
    e_in = jnp.take_along_axis(e_logits, grp[:, None, None], axis=1)[:, 0]
    top_v, top_i = lax.top_k(e_in, TOP_K)
    combine = p_grp * jax.nn.softmax(top_v, axis=-1)
    expert = grp[:, None] * EXPERTS_PER_GROUP + top_i

    A = T * TOP_K
    e_flat = expert.reshape(A)
    tok_flat = jnp.arange(A, dtype=jnp.int32) // TOP_K
    c_flat = combine.reshape(A)
    order = jnp.argsort(e_flat)
    e_sorted = e_flat[order]
    counts = jnp.bincount(e_flat, length=N_EXPERTS)
    padded = (counts + MOE_BLOCK - 1) // MOE_BLOCK * MOE_BLOCK
    starts = jnp.cumsum(counts) - counts
    p_ends = jnp.cumsum(padded)
    p_starts = p_ends - padded
    dest = p_starts[e_sorted] + (jnp.arange(A) - starts[e_sorted])
    P = A + N_EXPERTS * MOE_BLOCK
    n_blk = P // MOE_BLOCK
    tok_pad = jnp.full((P,), T, jnp.int32).at[dest].set(tok_flat[order])
    c_pad = jnp.zeros((P,), jnp.float32).at[dest].set(c_flat[order])
    blk_expert = jnp.clip(jnp.searchsorted(p_ends, jnp.arange(n_blk) * MOE_BLOCK, side='right'), 0, N_EXPERTS - 1)
    x_pad = jnp.concatenate([xt, jnp.zeros((1, D), xt.dtype)], axis=0)[tok_pad].reshape(n_blk, MOE_BLOCK, D)

    def expert_block(args):
        xb, e = args
        hb = jax.nn.silu(xb @ w_gate[e]) * (xb @ w_up[e])
        return hb @ w_down[e]

    y = lax.map(expert_block, (x_pad, blk_expert)).reshape(P, D).astype(jnp.float32) * c_pad[:, None]
    out = jax.ops.segment_sum(y, tok_pad, num_segments=T + 1)[:T]
    return out.astype(x.dtype).reshape(B, S, D)


def setup_inputs(seed: int = 0) -> dict:
    key = jax.random.key(seed)
    ks = jax.random.split(key, 28)
    L = DEPTH

    def nrm(k, shape, s):
        return jax.random.normal(k, shape, jnp.float32) * s

    def gain(k, n):
        return 1.0 + 0.05 * jax.random.normal(k, (L, n), jnp.float32)

    x = jax.random.normal(ks[0], (BATCH, SEQ, D_MODEL), jnp.float32)
    offset = jax.random.randint(ks[1], (BATCH, 1), 0, MAX_STREAM_OFFSET, dtype=jnp.int32)
    positions = offset + jnp.arange(SEQ, dtype=jnp.int32)[None, :]
    a0 = jax.random.uniform(ks[2], (L, D_RNN), jnp.float32, 0.9, 0.999)
    s0 = a0 ** (1.0 / LRU_C)
    lru_lambda = jnp.log(s0) - jnp.log1p(-s0)
    return {
        "x": x,
        "positions": positions,
        "norm_mix_g": gain(ks[3], D_MODEL),
        "w_in": nrm(ks[4], (L, D_MODEL, D_IN), D_MODEL ** -0.5),
        "conv_w": nrm(ks[5], (L, CONV_W, D_RNN), CONV_W ** -0.5),
        "conv_b": nrm(ks[6], (L, D_RNN), 0.02),
        "lru_wa": nrm(ks[7], (L, LRU_BLOCKS, LRU_BLOCK_W, LRU_BLOCK_W), LRU_BLOCK_W ** -0.5),
        "lru_ba": nrm(ks[8], (L, D_RNN), 0.02),
        "lru_wx": nrm(ks[9], (L, LRU_BLOCKS, LRU_BLOCK_W, LRU_BLOCK_W), LRU_BLOCK_W ** -0.5),
        "lru_bx": nrm(ks[10], (L, D_RNN), 0.02),
        "lru_lambda": lru_lambda,
        "q_a_g": gain(ks[11], Q_LORA),
        "w_uq": nrm(ks[12], (L, Q_LORA, MLA_HEADS * QK_HEAD), Q_LORA ** -0.5),
        "kv_a_g": gain(ks[13], KV_LORA),
        "w_ukv": nrm(ks[14], (L, KV_LORA, MLA_HEADS * (QK_NOPE + V_HEAD)), KV_LORA ** -0.5),
        "q_norm_g": gain(ks[15], QK_HEAD),
        "k_norm_g": gain(ks[16], QK_HEAD),
        "w_out": nrm(ks[17], (L, D_MODEL, D_MODEL), D_MODEL ** -0.5),
        "norm_ffn_g": gain(ks[18], D_MODEL),
        "router_group_w": nrm(ks[19], (L, D_MODEL, N_GROUPS), D_MODEL ** -0.5),
        "router_group_b": nrm(ks[20], (L, N_GROUPS), 0.01),
        "router_expert_w": nrm(ks[21], (L, D_MODEL, N_EXPERTS), D_MODEL ** -0.5),
        "router_expert_b": nrm(ks[22], (L, N_EXPERTS), 0.01),
        "w_gate": nrm(ks[23], (L, N_EXPERTS, D_MODEL, D_EXPERT), D_MODEL ** -0.5),
        "w_up": nrm(ks[24], (L, N_EXPERTS, D_MODEL, D_EXPERT), D_MODEL ** -0.5),
        "w_down": nrm(ks[25], (L, N_EXPERTS, D_EXPERT, D_MODEL), D_EXPERT ** -0.5),
    }


def reference(x, positions, norm_mix_g, w_in, conv_w, conv_b, lru_wa, lru_ba, lru_wx, lru_bx,
              lru_lambda, q_a_g, w_uq, kv_a_g, w_ukv, q_norm_g, k_norm_g, w_out, norm_ffn_g,
              router_group_w, router_group_b, router_expert_w, router_expert_b,
              w_gate, w_up, w_down):
    B, S, _ = x.shape
    split_points = np.cumsum(IN_SPLITS)[:-1].tolist()
    h = x
    for l in range(DEPTH):
        u = rms_norm(h, norm_mix_g[l])
        proj = u @ w_in[l]
        x_rnn, g_rnn, q_c, kv_c, k_pe, gate_logits = jnp.split(proj, split_points, axis=-1)
        xa = causal_conv(x_rnn, conv_w[l], conv_b[l])
        ya = rg_lru(xa, lru_wa[l], lru_ba[l], lru_wx[l], lru_bx[l], lru_lambda[l]) * jax.nn.gelu(g_rnn)
        yb = mla(q_c, kv_c, k_pe, positions, q_a_g[l], w_uq[l], kv_a_g[l], w_ukv[l], q_norm_g[l], k_norm_g[l])
        gates = jax.nn.sigmoid(gate_logits.astype(jnp.float32)).reshape(B, S, N_BRANCH, D_MODEL)
        merged = gates[:, :, 0] * ya.astype(jnp.float32) + gates[:, :, 1] * yb.astype(jnp.float32)
        h = h + merged.astype(h.dtype) @ w_out[l]
        h = h + hier_moe(rms_norm(h, norm_ffn_g[l]), router_group_w[l], router_group_b[l],
                         router_expert_w[l], router_expert_b[l], w_gate[l], w_up[l], w_down[l])
    return h
```

```python
import functools

import jax
import jax.numpy as jnp
import numpy as np
from jax import lax
from jax.experimental import pallas as pl
from jax.experimental.pallas import tpu as pltpu

F32 = jnp.float32
BF16 = jnp.bfloat16

D_MODEL = 1024
EPS = 1e-6
CHUNK = 64
D_RNN = 1024
LRU_BLOCKS = 4
LRU_BLOCK_W = 256
CONV_W = 4
LRU_C = 8.0
HEADS = 16
V_HEAD = 64
QK_NOPE = 64
QK_ROPE = 32
QK_HEAD = 96
Q_LORA = 768
KV_LORA = 256
ROPE_THETA = 10000.0
N_GROUPS = 4
EXPERTS_PER_GROUP = 8
N_EXPERTS = 32
D_EXPERT = 256
CHUNK_SHIFT = CHUNK.bit_length() - 1
GROUP_SHIFT = EXPERTS_PER_GROUP.bit_length() - 1

LANES = 128
TOK_TILE = 256
ATT_BLOCK = 256
QK_PAD = 128
V_AUG = 80
MOE_BLOCK = 128
NEG_BIG = -1e30
LOG2E = 1.4426950408889634
VMEM_LIMIT = 56 * 1024 * 1024

_SEG_XR, _SEG_GR, _SEG_QC, _SEG_KV, _SEG_PE, _SEG_GL = 0, 1024, 2048, 2816, 3072, 3200
_N_IN = 5248


def _cparams(*sem):
    return pltpu.CompilerParams(dimension_semantics=sem, vmem_limit_bytes=VMEM_LIMIT)


def _const_spec(shape):
    nd = len(shape)
    return pl.BlockSpec(shape, lambda *_: (0,) * nd)


def _inproj_kernel(x_ref, g_ref, w_ref, xr_ref, gr_ref, qc_ref, kv_ref, pe_ref, gl_ref):
    x = x_ref[...]
    ms = jnp.mean(x * x, axis=-1, keepdims=True)
    u = (x * lax.rsqrt(ms + EPS) * g_ref[...]).astype(BF16)
    p = jnp.dot(u, w_ref[...], preferred_element_type=F32)
    xr_ref[...] = p[:, _SEG_XR:_SEG_GR].astype(BF16)
    gr_ref[...] = p[:, _SEG_GR:_SEG_QC].astype(BF16)
    qc_ref[...] = p[:, _SEG_QC:_SEG_KV].astype(BF16)
    kv_ref[...] = p[:, _SEG_KV:_SEG_PE].astype(BF16)
    pe_ref[...] = p[:, _SEG_PE:_SEG_GL]
    gl_ref[...] = p[:, _SEG_GL:_N_IN].astype(BF16)


def _inproj(x2, g, w_cat):
    s = x2.shape[0]
    t = TOK_TILE
    row = lambda n: pl.BlockSpec((t, n), lambda i: (i, 0))
    return pl.pallas_call(
        _inproj_kernel,
        grid=(s // t,),
        in_specs=[row(D_MODEL), _const_spec((1, D_MODEL)), _const_spec((D_MODEL, _N_IN))],
        out_specs=[row(1024), row(1024), row(Q_LORA), row(KV_LORA), row(LANES), row(2048)],
        out_shape=[
            jax.ShapeDtypeStruct((s, 1024), BF16), jax.ShapeDtypeStruct((s, 1024), BF16),
            jax.ShapeDtypeStruct((s, Q_LORA), BF16), jax.ShapeDtypeStruct((s, KV_LORA), BF16),
            jax.ShapeDtypeStruct((s, LANES), F32), jax.ShapeDtypeStruct((s, 2048), BF16),
        ],
        compiler_params=_cparams("arbitrary"),
        name="inproj",
    )(x2, g, w_cat)


def _lru_kernel(xr_ref, gr_ref, ga_ref, cw_ref, cb_ref, wa_ref, ba_ref, wx_ref, bx_ref, lam_ref,
                out_ref, xbuf, hprev):
    t = TOK_TILE

    @pl.when(pl.program_id(0) == 0)
    def _():
        xbuf[0:8, :] = jnp.zeros((8, D_RNN), F32)
        hprev[...] = jnp.zeros((1, D_RNN), F32)

    x = xr_ref[...].astype(F32)
    xbuf[8:8 + t, :] = x
    xa = cb_ref[...] + cw_ref[3:4, :] * x
    for j in range(CONV_W - 1):
        xa = xa + cw_ref[j:j + 1, :] * xbuf[5 + j:5 + j + t, :]
    xbuf[0:8, :] = x[t - 8:t, :]

    r_parts, i_parts = [], []
    for b in range(LRU_BLOCKS):
        xb = xa[:, b * LRU_BLOCK_W:(b + 1) * LRU_BLOCK_W].astype(BF16)
        r_parts.append(jnp.dot(xb, wa_ref[b], preferred_element_type=F32))
        i_parts.append(jnp.dot(xb, wx_ref[b], preferred_element_type=F32))
    r = jax.nn.sigmoid(jnp.concatenate(r_parts, axis=1) + ba_ref[...])
    gi = jax.nn.sigmoid(jnp.concatenate(i_parts, axis=1) + bx_ref[...])
    z = -lam_ref[...]
    softplus = jnp.maximum(z, 0.0) + jnp.log(1.0 + jnp.exp(-jnp.abs(z)))
    log_a = (-LRU_C) * r * softplus
    a = jnp.exp(log_a)
    b_in = jnp.sqrt(1.0 - a * a) * (gi * xa)

    row = lax.broadcasted_iota(jnp.int32, (t, D_RNN), 0)
    d = 1
    while d < t:
        valid = row >= d
        a_sh = pltpu.roll(a, d, axis=0)
        b_sh = pltpu.roll(b_in, d, axis=0)
        b_in = jnp.where(valid, a * b_sh, 0.0) + b_in
        a = jnp.where(valid, a * a_sh, a)
        d *= 2
    h = a * hprev[...] + b_in
    hprev[...] = h[t - 1:t, :]

    ya = h * jax.nn.gelu(gr_ref[...].astype(F32))
    out_ref[...] = (jax.nn.sigmoid(ga_ref[...].astype(F32)) * ya).astype(BF16)


def _lru(x_rnn, g_rnn, gl, conv_w, conv_b, wa, ba, wx, bx, lam):
    s = x_rnn.shape[0]
    t = TOK_TILE
    row = pl.BlockSpec((t, 1024), lambda i: (i, 0))
    return pl.pallas_call(
        _lru_kernel,
        grid=(s // t,),
        in_specs=[row, row, row, _const_spec((CONV_W, D_RNN)), _const_spec((1, D_RNN)),
                  _const_spec((LRU_BLOCKS, LRU_BLOCK_W, LRU_BLOCK_W)), _const_spec((1, D_RNN)),
                  _const_spec((LRU_BLOCKS, LRU_BLOCK_W, LRU_BLOCK_W)), _const_spec((1, D_RNN)),
                  _const_spec((1, D_RNN))],
        out_specs=row,
        out_shape=jax.ShapeDtypeStruct((s, D_RNN), BF16),
        scratch_shapes=[pltpu.VMEM((8 + t, D_RNN), F32), pltpu.VMEM((1, D_RNN), F32)],
        compiler_params=_cparams("arbitrary"),
        name="lru",
    )(x_rnn, g_rnn, gl, conv_w, conv_b, wa, ba, wx, bx, lam)


def _rope_rows(x, cos, sin):
    half = QK_ROPE // 2
    x1, x2 = x[:, :half, :], x[:, half:, :]
    return x1 * cos - x2 * sin, x1 * sin + x2 * cos


def _mla_prep_kernel(qc_ref, kv_ref, pe_ref, pos_ref, freq_ref, gq_ref, gkv_ref, qg_ref, kg_ref,
                     wq_ref, wk_ref, wv_ref, q_out, k_out, v_out):
    t = TOK_TILE
    ang = pos_ref[...].astype(F32) * freq_ref[...]
    cos, sin = jnp.cos(ang), jnp.sin(ang)

    qct = qc_ref[...].astype(F32).T
    qcn = qct * lax.rsqrt(jnp.mean(qct * qct, axis=0, keepdims=True) + EPS) * gq_ref[...]
    q = jnp.dot(wq_ref[...], qcn.astype(BF16), preferred_element_type=F32)
    q = q.reshape(HEADS, QK_HEAD, t)
    q = q * lax.rsqrt(jnp.mean(q * q, axis=1, keepdims=True) + EPS) * qg_ref[...][None]
    r1, r2 = _rope_rows(q[:, QK_NOPE:, :], cos, sin)
    scale = (QK_HEAD ** -0.5) * LOG2E
    qn = jnp.concatenate([q[:, :QK_NOPE, :], r1, r2, jnp.zeros((HEADS, QK_PAD - QK_HEAD, t), F32)], axis=1)
    q_out[...] = (qn * scale).astype(BF16)

    kvt = kv_ref[...].astype(F32).T
    kvn = (kvt * lax.rsqrt(jnp.mean(kvt * kvt, axis=0, keepdims=True) + EPS) * gkv_ref[...]).astype(BF16)
    kn = jnp.dot(wk_ref[...], kvn, preferred_element_type=F32).reshape(HEADS, QK_NOPE, t)
    v = jnp.dot(wv_ref[...], kvn, preferred_element_type=F32).reshape(HEADS, V_HEAD, t)
    pe = pe_ref[...].T[:QK_ROPE, :]
    ssq = jnp.sum(kn * kn, axis=1, keepdims=True) + jnp.sum(pe * pe, axis=0, keepdims=True)[None]
    rstd = lax.rsqrt(ssq * (1.0 / QK_HEAD) + EPS)
    kg = kg_ref[...]
    k_nope = kn * rstd * kg[None, :QK_NOPE, :]
    k_pe = (pe * kg[QK_NOPE:, :])[None] * rstd
    p1, p2 = _rope_rows(k_pe, cos, sin)
    kt = jnp.concatenate([k_nope, p1, p2, jnp.zeros((HEADS, QK_PAD - QK_HEAD, t), F32)], axis=1)
    for h in range(HEADS):
        k_out[h] = kt[h].T.astype(BF16)
    v_out[...] = jnp.concatenate([v, jnp.ones((HEADS, V_AUG - V_HEAD, t), F32)], axis=1).astype(BF16)


def _mla_prep(q_c, kv_c, pe, pos_row, freq, gq, gkv, qg, kg, wq_t, wk_t, wv_t):
    s = q_c.shape[0]
    t = TOK_TILE
    nt = s // t
    row = lambda n: pl.BlockSpec((t, n), lambda i: (i, 0))
    blk = lambda r, c: pl.BlockSpec((HEADS, None, r, c), lambda i: (0, i, 0, 0))
    return pl.pallas_call(
        _mla_prep_kernel,
        grid=(nt,),
        in_specs=[row(Q_LORA), row(KV_LORA), row(LANES),
                  pl.BlockSpec((None, 1, t), lambda i: (i, 0, 0)),
                  _const_spec((QK_ROPE // 2, 1)), _const_spec((Q_LORA, 1)), _const_spec((KV_LORA, 1)),
                  _const_spec((QK_HEAD, 1)), _const_spec((QK_HEAD, 1)),
                  _const_spec((HEADS * QK_HEAD, Q_LORA)), _const_spec((HEADS * QK_NOPE, KV_LORA)),
                  _const_spec((HEADS * V_HEAD, KV_LORA))],
        out_specs=[blk(QK_PAD, t), blk(t, QK_PAD), blk(V_AUG, t)],
        out_shape=[jax.ShapeDtypeStruct((HEADS, nt, QK_PAD, t), BF16),
                   jax.ShapeDtypeStruct((HEADS, nt, t, QK_PAD), BF16),
                   jax.ShapeDtypeStruct((HEADS, nt, V_AUG, t), BF16)],
        compiler_params=_cparams("arbitrary"),
        name="mla_prep",
    )(q_c, kv_c, pe, pos_row, freq, gq, gkv, qg, kg, wq_t, wk_t, wv_t)


def _attn_kernel(q_ref, k_ref, v_ref, o_ref):
    tb = ATT_BLOCK
    qi = pl.program_id(1)
    q = q_ref[...]

    def step(j, carry, masked):
        m, acc = carry
        s = jnp.dot(k_ref[j], q, preferred_element_type=F32)
        if masked:
            kk = lax.broadcasted_iota(jnp.int32, (tb, tb), 0) >> CHUNK_SHIFT
            qq = lax.broadcasted_iota(jnp.int32, (tb, tb), 1) >> CHUNK_SHIFT
            s = jnp.where(kk <= qq, s, NEG_BIG)
        m_new = jnp.maximum(m, jnp.max(s, axis=0, keepdims=True))
        p = jnp.exp2(s - m_new).astype(BF16)
        acc = acc * jnp.exp2(m - m_new) + jnp.dot(v_ref[j], p, preferred_element_type=F32)
        return m_new, acc

    init = (jnp.full((1, tb), NEG_BIG, F32), jnp.zeros((V_AUG, tb), F32))
    carry = lax.fori_loop(0, qi, lambda j, c: step(j, c, False), init)
    _, acc = step(qi, carry, True)
    o_ref[...] = acc[:V_HEAD, :] / acc[V_HEAD:V_HEAD + 1, :]


def _attention(q_t, k, v_t):
    heads, nb = q_t.shape[0], q_t.shape[1]
    tb = ATT_BLOCK
    return pl.pallas_call(
        _attn_kernel,
        grid=(heads, nb),
        in_specs=[pl.BlockSpec((None, None, QK_PAD, tb), lambda h, i: (h, i, 0, 0)),
                  pl.BlockSpec((None, nb, tb, QK_PAD), lambda h, i: (h, 0, 0, 0)),
                  pl.BlockSpec((None, nb, V_AUG, tb), lambda h, i: (h, 0, 0, 0))],
        out_specs=pl.BlockSpec((None, None, V_HEAD, tb), lambda h, i: (h, i, 0, 0)),
        out_shape=jax.ShapeDtypeStruct((heads, nb, V_HEAD, tb), F32),
        compiler_params=_cparams("arbitrary", "arbitrary"),
        name="attention",
    )(q_t, k, v_t)


def _first_lane(cond, lane):
    return jnp.min(jnp.where(cond, lane.astype(F32), 4.0 * LANES), axis=1, keepdims=True).astype(jnp.int32)


def _outproj_router_kernel(x_ref, ya_ref, gb_ref, yb_ref, wo_ref, gf_ref, wrh_ref, wrl_ref, br_ref,
                           h_ref, xn_ref, ri_ref, rf_ref, cnt_ref, base):
    t = TOK_TILE

    @pl.when(pl.program_id(0) == 0)
    def _():
        base[...] = jnp.zeros((1, LANES), F32)

    yb = yb_ref[...].reshape(HEADS * V_HEAD, t).T
    merged = ya_ref[...].astype(F32) + jax.nn.sigmoid(gb_ref[...].astype(F32)) * yb
    h = x_ref[...] + jnp.dot(merged.astype(BF16), wo_ref[...], preferred_element_type=F32)
    h_ref[...] = h
    xn = h * lax.rsqrt(jnp.mean(h * h, axis=-1, keepdims=True) + EPS) * gf_ref[...]
    xn_ref[...] = xn

    x_hi = xn.astype(BF16)
    x_lo = (xn - x_hi.astype(F32)).astype(BF16)
    logits = (jnp.dot(x_hi, wrh_ref[...], preferred_element_type=F32)
              + jnp.dot(x_hi, wrl_ref[...], preferred_element_type=F32)
              + jnp.dot(x_lo, wrh_ref[...], preferred_element_type=F32)) + br_ref[...]

    lane = lax.broadcasted_iota(jnp.int32, (t, LANES), 1)
    is_grp = jnp.logical_and(lane >= N_EXPERTS, lane < N_EXPERTS + N_GROUPS)
    gl = jnp.where(is_grp, logits, NEG_BIG)
    gmax = jnp.max(gl, axis=1, keepdims=True)
    grp = _first_lane(gl == gmax, lane) - N_EXPERTS
    p_grp = 1.0 / jnp.sum(jnp.where(is_grp, jnp.exp(gl - gmax), 0.0), axis=1, keepdims=True)
    el = jnp.where((lane >> GROUP_SHIFT) == grp, logits, NEG_BIG)
    v1 = jnp.max(el, axis=1, keepdims=True)
    e1 = _first_lane(el == v1, lane)
    el2 = jnp.where(lane == e1, NEG_BIG, el)
    v2 = jnp.max(el2, axis=1, keepdims=True)
    e2 = _first_lane(el2 == v2, lane)
    ex = jnp.exp(v2 - v1)
    w1 = p_grp / (1.0 + ex)
    w2 = p_grp * ex / (1.0 + ex)

    tri = (lax.broadcasted_iota(jnp.int32, (t, t), 1) < lax.broadcasted_iota(jnp.int32, (t, t), 0))
    tri = jnp.where(tri, 1.0, 0.0).astype(BF16)
    oh1 = jnp.where(lane == e1, 1.0, 0.0)
    oh2 = jnp.where(lane == e2, 1.0, 0.0)
    pre1 = jnp.dot(tri, oh1.astype(BF16), preferred_element_type=F32) + base[...]
    base1 = base[...] + jnp.sum(oh1, axis=0, keepdims=True)
    pre2 = jnp.dot(tri, oh2.astype(BF16), preferred_element_type=F32) + base1
    base2 = base1 + jnp.sum(oh2, axis=0, keepdims=True)
    r1 = jnp.sum(oh1 * pre1, axis=1, keepdims=True).astype(jnp.int32)
    r2 = jnp.sum(oh2 * pre2, axis=1, keepdims=True).astype(jnp.int32)
    base[...] = base2
    cnt_ref[...] = base2

    ri_ref[...] = jnp.where(lane == 0, e1, jnp.where(lane == 1, e2, jnp.where(lane == 2, r1, r2)))
    rf_ref[...] = jnp.where(lane == 0, w1, w2)


def _outproj_router(x2, ya_g, gl, yb_t, w_out, gf, wr_hi, wr_lo, br):
    s = x2.shape[0]
    t = TOK_TILE
    row = lambda n: pl.BlockSpec((t, n), lambda i: (i, 0))
    return pl.pallas_call(
        _outproj_router_kernel,
        grid=(s // t,),
        in_specs=[row(D_MODEL), row(D_MODEL), pl.BlockSpec((t, D_MODEL), lambda i: (i, 1)),
                  pl.BlockSpec((HEADS, None, V_HEAD, t), lambda i: (0, i, 0, 0)),
                  _const_spec((D_MODEL, D_MODEL)), _const_spec((1, D_MODEL)),
                  _const_spec((D_MODEL, LANES)), _const_spec((D_MODEL, LANES)), _const_spec((1, LANES))],
        out_specs=[row(D_MODEL), row(D_MODEL), row(LANES), row(LANES), _const_spec((1, LANES))],
        out_shape=[jax.ShapeDtypeStruct((s, D_MODEL), F32), jax.ShapeDtypeStruct((s, D_MODEL), F32),
                   jax.ShapeDtypeStruct((s, LANES), jnp.int32), jax.ShapeDtypeStruct((s, LANES), F32),
                   jax.ShapeDtypeStruct((1, LANES), F32)],
        scratch_shapes=[pltpu.VMEM((1, LANES), F32)],
        compiler_params=_cparams("arbitrary"),
        name="outproj_router",
    )(x2, ya_g, gl, yb_t, w_out, gf, wr_hi, wr_lo, br)


def _row_copy(src_ref, src_row, dst_ref, dst_row, sem):
    return pltpu.make_async_copy(src_ref.at[pl.ds(src_row, 1), :], dst_ref.at[pl.ds(dst_row, 1), :], sem)


def _scatter_kernel(d1_ref, d2_ref, xn_ref, zeros_ref, xpad_ref, sem):
    del zeros_ref
    t = TOK_TILE

    def issue(i, c):
        _row_copy(xn_ref, i, xpad_ref, d1_ref[0, i], sem).start()
        _row_copy(xn_ref, i, xpad_ref, d2_ref[0, i], sem).start()
        return c

    lax.fori_loop(0, t, issue, 0)

    def drain(i, c):
        _row_copy(xn_ref, 0, xpad_ref, 0, sem).wait()
        _row_copy(xn_ref, 0, xpad_ref, 0, sem).wait()
        return c

    lax.fori_loop(0, t, drain, 0)


def _scatter_rows(d1, d2, xn, xpad_zeros):
    s = xn.shape[0]
    t = TOK_TILE
    idx = pl.BlockSpec((None, 1, t), lambda i: (i, 0, 0), memory_space=pltpu.SMEM)
    return pl.pallas_call(
        _scatter_kernel,
        grid=(s // t,),
        in_specs=[idx, idx, pl.BlockSpec((t, D_MODEL), lambda i: (i, 0)), pl.BlockSpec(memory_space=pl.ANY)],
        out_specs=pl.BlockSpec(memory_space=pl.ANY),
        out_shape=jax.ShapeDtypeStruct(xpad_zeros.shape, F32),
        scratch_shapes=[pltpu.SemaphoreType.DMA(())],
        input_output_aliases={3: 0},
        compiler_params=_cparams("arbitrary"),
        name="moe_scatter",
    )(d1, d2, xn, xpad_zeros)


def _expert_kernel(be_ref, na_ref, x_ref, wgu_ref, wd_ref, y_ref):
    del be_ref
    active = pl.program_id(0) < na_ref[0]

    @pl.when(active)
    def _():
        x = x_ref[...].astype(BF16)
        gu = jnp.dot(x, wgu_ref[...], preferred_element_type=F32)
        g, u = gu[:, :D_EXPERT], gu[:, D_EXPERT:]
        hid = (g * jax.nn.sigmoid(g) * u).astype(BF16)
        y_ref[...] = jnp.dot(hid, wd_ref[...], preferred_element_type=F32)

    @pl.when(jnp.logical_not(active))
    def _():
        y_ref[...] = jnp.zeros_like(y_ref)


def _experts(blk_expert, n_active, xpad, w_gu, w_down):
    nblk = xpad.shape[0] // MOE_BLOCK
    rows = lambda b, be, na: (jnp.minimum(b, na[0] - 1), 0)
    grid_spec = pltpu.PrefetchScalarGridSpec(
        num_scalar_prefetch=2,
        grid=(nblk,),
        in_specs=[pl.BlockSpec((MOE_BLOCK, D_MODEL), rows),
                  pl.BlockSpec((None, D_MODEL, 2 * D_EXPERT), lambda b, be, na: (be[b], 0, 0)),
                  pl.BlockSpec((None, D_EXPERT, D_MODEL), lambda b, be, na: (be[b], 0, 0))],
        out_specs=pl.BlockSpec((MOE_BLOCK, D_MODEL), lambda b, be, na: (b, 0)),
    )
    return pl.pallas_call(
        _expert_kernel,
        grid_spec=grid_spec,
        out_shape=jax.ShapeDtypeStruct(xpad.shape, F32),
        compiler_params=_cparams("arbitrary"),
        name="moe_experts",
    )(blk_expert, n_active, xpad, w_gu, w_down)


def _combine_kernel(d1_ref, d2_ref, h_ref, rf_ref, ypad_ref, out_ref, ybuf, sem):
    t = TOK_TILE

    def issue(i, c):
        _row_copy(ypad_ref, d1_ref[0, i], ybuf.at[0], i, sem).start()
        _row_copy(ypad_ref, d2_ref[0, i], ybuf.at[1], i, sem).start()
        return c

    lax.fori_loop(0, t, issue, 0)

    def drain(i, c):
        _row_copy(ypad_ref, 0, ybuf.at[0], 0, sem).wait()
        _row_copy(ypad_ref, 0, ybuf.at[1], 0, sem).wait()
        return c

    lax.fori_loop(0, t, drain, 0)
    rf = rf_ref[...]
    out_ref[...] = h_ref[...] + rf[:, 0:1] * ybuf[0] + rf[:, 1:2] * ybuf[1]


def _combine(d1, d2, h, rf, ypad):
    s = h.shape[0]
    t = TOK_TILE
    idx = pl.BlockSpec((None, 1, t), lambda i: (i, 0, 0), memory_space=pltpu.SMEM)
    return pl.pallas_call(
        _combine_kernel,
        grid=(s // t,),
        in_specs=[idx, idx, pl.BlockSpec((t, D_MODEL), lambda i: (i, 0)),
                  pl.BlockSpec((t, LANES), lambda i: (i, 0)), pl.BlockSpec(memory_space=pl.ANY)],
        out_specs=pl.BlockSpec((t, D_MODEL), lambda i: (i, 0)),
        out_shape=jax.ShapeDtypeStruct((s, D_MODEL), F32),
        scratch_shapes=[pltpu.VMEM((2, t, D_MODEL), F32), pltpu.SemaphoreType.DMA(())],
        compiler_params=_cparams("arbitrary"),
        name="moe_combine",
    )(d1, d2, h, rf, ypad)


def _layer(h3, positions, norm_mix_g, w_in, conv_w, conv_b, lru_wa, lru_ba, lru_wx, lru_bx, lru_lambda,
           q_a_g, w_uq, kv_a_g, w_ukv, q_norm_g, k_norm_g, w_out, norm_ffn_g,
           router_group_w, router_group_b, router_expert_w, router_expert_b, w_gate, w_up, w_down):
    b, s, _ = h3.shape
    assert b == 1 and s % TOK_TILE == 0 and TOK_TILE == ATT_BLOCK
    x2 = h3.reshape(s, D_MODEL)
    nt = s // TOK_TILE

    o = np.cumsum((0, D_RNN, D_RNN, Q_LORA, KV_LORA, QK_ROPE))
    w_cat = jnp.concatenate(
        [w_in[:, o[0]:o[4]], w_in[:, o[4]:o[5]], jnp.zeros((D_MODEL, LANES - QK_ROPE), F32), w_in[:, o[5]:]],
        axis=1).astype(BF16)
    row = lambda v: v.reshape(1, -1)
    col = lambda v: v.reshape(-1, 1)
    wq_t = w_uq.T.astype(BF16)
    ukv = w_ukv.reshape(KV_LORA, HEADS, QK_NOPE + V_HEAD)
    wk_t = ukv[:, :, :QK_NOPE].reshape(KV_LORA, HEADS * QK_NOPE).T.astype(BF16)
    wv_t = ukv[:, :, QK_NOPE:].reshape(KV_LORA, HEADS * V_HEAD).T.astype(BF16)
    half = QK_ROPE // 2
    freq = col(ROPE_THETA ** (-jnp.arange(half, dtype=F32) / half))
    wr = jnp.concatenate([router_expert_w, router_group_w,
                          jnp.zeros((D_MODEL, LANES - N_EXPERTS - N_GROUPS), F32)], axis=1)
    wr_hi = wr.astype(BF16)
    wr_lo = (wr - wr_hi.astype(F32)).astype(BF16)
    br = row(jnp.concatenate([router_expert_b, router_group_b, jnp.zeros((LANES - N_EXPERTS - N_GROUPS,), F32)]))
    w_gu = jnp.concatenate([w_gate, w_up], axis=2).astype(BF16)

    x_rnn, g_rnn, q_c, kv_c, pe, gl = _inproj(x2, row(norm_mix_g), w_cat)
    ya_g = _lru(x_rnn, g_rnn, gl, conv_w, row(conv_b), lru_wa.astype(BF16), row(lru_ba),
                lru_wx.astype(BF16), row(lru_bx), row(lru_lambda))
    q_t, k_r, v_t = _mla_prep(q_c, kv_c, pe, positions.reshape(nt, 1, TOK_TILE), freq, col(q_a_g), col(kv_a_g),
                              col(q_norm_g), col(k_norm_g), wq_t, wk_t, wv_t)
    yb_t = _attention(q_t, k_r, v_t)
    h, xn, ri, rf, counts = _outproj_router(x2, ya_g, gl, yb_t, w_out.astype(BF16), row(norm_ffn_g),
                                            wr_hi, wr_lo, br)

    cnt = counts[0, :N_EXPERTS].astype(jnp.int32)
    padded = (cnt + MOE_BLOCK - 1) // MOE_BLOCK * MOE_BLOCK
    p_end = jnp.cumsum(padded)
    p_start = p_end - padded
    n_slots = 2 * s + N_EXPERTS * MOE_BLOCK
    nblk = n_slots // MOE_BLOCK
    blk_first = jnp.arange(nblk, dtype=jnp.int32) * MOE_BLOCK
    blk_expert = jnp.minimum(jnp.sum((p_end[None, :] <= blk_first[:, None]).astype(jnp.int32), axis=1),
                             N_EXPERTS - 1)
    n_active = (p_end[-1:] // MOE_BLOCK).astype(jnp.int32)
    d1 = (p_start[ri[:, 0]] + ri[:, 2]).reshape(nt, 1, TOK_TILE)
    d2 = (p_start[ri[:, 1]] + ri[:, 3]).reshape(nt, 1, TOK_TILE)

    xpad = _scatter_rows(d1, d2, xn, jnp.zeros((n_slots, D_MODEL), F32))
    ypad = _experts(blk_expert, n_active, xpad, w_gu, w_down.astype(BF16))
    out = _combine(d1, d2, h, rf, ypad)
    return out.reshape(b, s, D_MODEL)


def kernel(x, positions, norm_mix_g, w_in, conv_w, conv_b, lru_wa, lru_ba, lru_wx, lru_bx, lru_lambda, q_a_g,
           w_uq, kv_a_g, w_ukv, q_norm_g, k_norm_g, w_out, norm_ffn_g, router_group_w, router_group_b,
           router_expert_w, router_expert_b, w_gate, w_up, w_down):
    h = x
    for l in range(norm_mix_g.shape[0]):
        h = _layer(h, positions, norm_mix_g[l], w_in[l], conv_w[l], conv_b[l], lru_wa[l], lru_ba[l], lru_wx[l],
                   lru_bx[l], lru_lambda[l], q_a_g[l], w_uq[l], kv_a_g[l], w_ukv[l], q_norm_g[l], k_norm_g[l],
                   w_out[l], norm_ffn_g[l], router_group_w[l], router_group_b[l], router_expert_w[l],
                   router_expert_b[l], w_gate[l], w_up[l], w_down[l])
    return h
```

```python
import functools

import jax
import jax.numpy as jnp
import numpy as np
from jax import lax
from jax.experimental import pallas as pl
from jax.experimental.pallas import tpu as pltpu

F32 = jnp.float32
BF16 = jnp.bfloat16

D_MODEL = 1024
EPS = 1e-6
CHUNK = 64
D_RNN = 1024
LRU_BLOCKS = 4
LRU_BLOCK_W = 256
CONV_W = 4
LRU_C = 8.0
HEADS = 16
V_HEAD = 64
QK_NOPE = 64
QK_ROPE = 32
QK_HEAD = 96
Q_LORA = 768
KV_LORA = 256
ROPE_THETA = 10000.0
N_GROUPS = 4
EXPERTS_PER_GROUP = 8
N_EXPERTS = 32
D_EXPERT = 256
CHUNK_SHIFT = CHUNK.bit_length() - 1
GROUP_SHIFT = EXPERTS_PER_GROUP.bit_length() - 1

LANES = 128
TOK_TILE = 256
ATT_TQ = 512
ATT_HEADS = 4
QK_PAD = 128
V_AUG = 80
MOE_BLOCK = 128
NEG_BIG = -1e30
LOG2E = 1.4426950408889634
VMEM_LIMIT = 56 * 1024 * 1024

_SEG_XR, _SEG_GR, _SEG_QC, _SEG_KV, _SEG_PE, _SEG_GL = 0, 1024, 2048, 2816, 3072, 3200
_N_IN = 5248


def _cparams(*sem):
    return pltpu.CompilerParams(dimension_semantics=sem, vmem_limit_bytes=VMEM_LIMIT)


def _const_spec(shape):
    nd = len(shape)
    return pl.BlockSpec(shape, lambda *_: (0,) * nd)


def _inproj_kernel(x_ref, g_ref, w_ref, xr_ref, gr_ref, qc_ref, kv_ref, pe_ref, gl_ref):
    x = x_ref[...]
    ms = jnp.mean(x * x, axis=-1, keepdims=True)
    u = (x * lax.rsqrt(ms + EPS) * g_ref[...]).astype(BF16)
    p = jnp.dot(u, w_ref[...], preferred_element_type=F32)
    xr_ref[...] = p[:, _SEG_XR:_SEG_GR].astype(BF16)
    gr_ref[...] = p[:, _SEG_GR:_SEG_QC].astype(BF16)
    qc_ref[...] = p[:, _SEG_QC:_SEG_KV].astype(BF16)
    kv_ref[...] = p[:, _SEG_KV:_SEG_PE].astype(BF16)
    pe_ref[...] = p[:, _SEG_PE:_SEG_GL]
    gl_ref[...] = p[:, _SEG_GL:_N_IN].astype(BF16)


def _inproj(x2, g, w_cat):
    s = x2.shape[0]
    t = TOK_TILE
    row = lambda n: pl.BlockSpec((t, n), lambda i: (i, 0))
    return pl.pallas_call(
        _inproj_kernel,
        grid=(s // t,),
        in_specs=[row(D_MODEL), _const_spec((1, D_MODEL)), _const_spec((D_MODEL, _N_IN))],
        out_specs=[row(1024), row(1024), row(Q_LORA), row(KV_LORA), row(LANES), row(2048)],
        out_shape=[
            jax.ShapeDtypeStruct((s, 1024), BF16), jax.ShapeDtypeStruct((s, 1024), BF16),
            jax.ShapeDtypeStruct((s, Q_LORA), BF16), jax.ShapeDtypeStruct((s, KV_LORA), BF16),
            jax.ShapeDtypeStruct((s, LANES), F32), jax.ShapeDtypeStruct((s, 2048), BF16),
        ],
        compiler_params=_cparams("arbitrary"),
        name="inproj",
    )(x2, g, w_cat)


def _lru_kernel(xr_ref, gr_ref, ga_ref, cw_ref, cb_ref, wa_ref, ba_ref, wx_ref, bx_ref, lam_ref,
                out_ref, xbuf, hprev):
    t = TOK_TILE

    @pl.when(pl.program_id(0) == 0)
    def _():
        xbuf[0:8, :] = jnp.zeros((8, D_RNN), F32)
        hprev[...] = jnp.zeros((1, D_RNN), F32)

    x = xr_ref[...].astype(F32)
    xbuf[8:8 + t, :] = x
    xa = cb_ref[...] + cw_ref[3:4, :] * x
    for j in range(CONV_W - 1):
        xa = xa + cw_ref[j:j + 1, :] * xbuf[5 + j:5 + j + t, :]
    xbuf[0:8, :] = x[t - 8:t, :]

    r_parts, i_parts = [], []
    for b in range(LRU_BLOCKS):
        xb = xa[:, b * LRU_BLOCK_W:(b + 1) * LRU_BLOCK_W].astype(BF16)
        r_parts.append(jnp.dot(xb, wa_ref[b], preferred_element_type=F32))
        i_parts.append(jnp.dot(xb, wx_ref[b], preferred_element_type=F32))
    r = jax.nn.sigmoid(jnp.concatenate(r_parts, axis=1) + ba_ref[...])
    gi = jax.nn.sigmoid(jnp.concatenate(i_parts, axis=1) + bx_ref[...])
    z = -lam_ref[...]
    softplus = jnp.maximum(z, 0.0) + jnp.log(1.0 + jnp.exp(-jnp.abs(z)))
    log_a = (-LRU_C) * r * softplus
    a = jnp.exp(log_a)
    b_in = jnp.sqrt(1.0 - a * a) * (gi * xa)

    row = lax.broadcasted_iota(jnp.int32, (t, D_RNN), 0)
    d = 1
    while d < t:
        valid = row >= d
        a_sh = pltpu.roll(a, d, axis=0)
        b_sh = pltpu.roll(b_in, d, axis=0)
        b_in = jnp.where(valid, a * b_sh, 0.0) + b_in
        a = jnp.where(valid, a * a_sh, a)
        d *= 2
    h = a * hprev[...] + b_in
    hprev[...] = h[t - 1:t, :]

    ya = h * jax.nn.gelu(gr_ref[...].astype(F32))
    out_ref[...] = (jax.nn.sigmoid(ga_ref[...].astype(F32)) * ya).astype(BF16)


def _lru(x_rnn, g_rnn, gl, conv_w, conv_b, wa, ba, wx, bx, lam):
    s = x_rnn.shape[0]
    t = TOK_TILE
    row = pl.BlockSpec((t, 1024), lambda i: (i, 0))
    return pl.pallas_call(
        _lru_kernel,
        grid=(s // t,),
        in_specs=[row, row, row, _const_spec((CONV_W, D_RNN)), _const_spec((1, D_RNN)),
                  _const_spec((LRU_BLOCKS, LRU_BLOCK_W, LRU_BLOCK_W)), _const_spec((1, D_RNN)),
                  _const_spec((LRU_BLOCKS, LRU_BLOCK_W, LRU_BLOCK_W)), _const_spec((1, D_RNN)),
                  _const_spec((1, D_RNN))],
        out_specs=row,
        out_shape=jax.ShapeDtypeStruct((s, D_RNN), BF16),
        scratch_shapes=[pltpu.VMEM((8 + t, D_RNN), F32), pltpu.VMEM((1, D_RNN), F32)],
        compiler_params=_cparams("arbitrary"),
        name="lru",
    )(x_rnn, g_rnn, gl, conv_w, conv_b, wa, ba, wx, bx, lam)


def _rope_rows(x, cos, sin):
    half = QK_ROPE // 2
    x1, x2 = x[:, :half, :], x[:, half:, :]
    return x1 * cos - x2 * sin, x1 * sin + x2 * cos


def _mla_prep_kernel(qc_ref, kv_ref, pe_ref, pos_ref, freq_ref, gq_ref, gkv_ref, qg_ref, kg_ref,
                     wq_ref, wk_ref, wv_ref, q_out, k_out, v_out):
    t = TOK_TILE
    ang = pos_ref[...].astype(F32) * freq_ref[...]
    cos, sin = jnp.cos(ang), jnp.sin(ang)

    qct = qc_ref[...].astype(F32).T
    qcn = qct * lax.rsqrt(jnp.mean(qct * qct, axis=0, keepdims=True) + EPS) * gq_ref[...]
    q = jnp.dot(wq_ref[...], qcn.astype(BF16), preferred_element_type=F32)
    q = q.reshape(HEADS, QK_HEAD, t)
    q = q * lax.rsqrt(jnp.mean(q * q, axis=1, keepdims=True) + EPS) * qg_ref[...][None]
    r1, r2 = _rope_rows(q[:, QK_NOPE:, :], cos, sin)
    scale = (QK_HEAD ** -0.5) * LOG2E
    qn = jnp.concatenate([q[:, :QK_NOPE, :], r1, r2, jnp.zeros((HEADS, QK_PAD - QK_HEAD, t), F32)], axis=1)
    q_out[...] = (qn * scale).astype(BF16)

    kvt = kv_ref[...].astype(F32).T
    kvn = (kvt * lax.rsqrt(jnp.mean(kvt * kvt, axis=0, keepdims=True) + EPS) * gkv_ref[...]).astype(BF16)
    kn = jnp.dot(wk_ref[...], kvn, preferred_element_type=F32).reshape(HEADS, QK_NOPE, t)
    v = jnp.dot(wv_ref[...], kvn, preferred_element_type=F32).reshape(HEADS, V_HEAD, t)
    pe = pe_ref[...].T[:QK_ROPE, :]
    ssq = jnp.sum(kn * kn, axis=1, keepdims=True) + jnp.sum(pe * pe, axis=0, keepdims=True)[None]
    rstd = lax.rsqrt(ssq * (1.0 / QK_HEAD) + EPS)
    kg = kg_ref[...]
    k_nope = kn * rstd * kg[None, :QK_NOPE, :]
    k_pe = (pe * kg[QK_NOPE:, :])[None] * rstd
    p1, p2 = _rope_rows(k_pe, cos, sin)
    kt = jnp.concatenate([k_nope, p1, p2, jnp.zeros((HEADS, QK_PAD - QK_HEAD, t), F32)], axis=1)
    for h in range(HEADS):
        k_out[h] = kt[h].T.astype(BF16)
    v_out[...] = jnp.concatenate([v, jnp.ones((HEADS, V_AUG - V_HEAD, t), F32)], axis=1).astype(BF16)


def _mla_prep(q_c, kv_c, pe, pos_row, freq, gq, gkv, qg, kg, wq_t, wk_t, wv_t):
    s = q_c.shape[0]
    t = TOK_TILE
    nt = s // t
    row = lambda n: pl.BlockSpec((t, n), lambda i: (i, 0))
    blk = lambda r, c: pl.BlockSpec((HEADS, None, r, c), lambda i: (0, i, 0, 0))
    return pl.pallas_call(
        _mla_prep_kernel,
        grid=(nt,),
        in_specs=[row(Q_LORA), row(KV_LORA), row(LANES),
                  pl.BlockSpec((None, 1, t), lambda i: (i, 0, 0)),
                  _const_spec((QK_ROPE // 2, 1)), _const_spec((Q_LORA, 1)), _const_spec((KV_LORA, 1)),
                  _const_spec((QK_HEAD, 1)), _const_spec((QK_HEAD, 1)),
                  _const_spec((HEADS * QK_HEAD, Q_LORA)), _const_spec((HEADS * QK_NOPE, KV_LORA)),
                  _const_spec((HEADS * V_HEAD, KV_LORA))],
        out_specs=[blk(QK_PAD, t), blk(t, QK_PAD), blk(V_AUG, t)],
        out_shape=[jax.ShapeDtypeStruct((HEADS, nt, QK_PAD, t), BF16),
                   jax.ShapeDtypeStruct((HEADS, nt, t, QK_PAD), BF16),
                   jax.ShapeDtypeStruct((HEADS, nt, V_AUG, t), BF16)],
        compiler_params=_cparams("arbitrary"),
        name="mla_prep",
    )(q_c, kv_c, pe, pos_row, freq, gq, gkv, qg, kg, wq_t, wk_t, wv_t)


def _attn_kernel(q_ref, k_ref, v_ref, o_ref, s_scr):
    tq, tk = ATT_TQ, TOK_TILE
    nsub = tq // tk
    qi = pl.program_id(1)
    qs = [jnp.concatenate([q_ref[g, a] for a in range(nsub)], axis=1) for g in range(ATT_HEADS)]

    def prefetch(kb, slot):
        for g in range(ATT_HEADS):
            s_scr[slot, g] = jnp.dot(k_ref[g, kb], qs[g], preferred_element_type=F32)

    def consume(kb, slot, state, tail):
        out = []
        for g in range(ATT_HEADS):
            m, acc = state[g]
            if tail is None:
                visible = None
            else:
                kk = (lax.broadcasted_iota(jnp.int32, (tk, tq), 0) + tail * tk) >> CHUNK_SHIFT
                qq = lax.broadcasted_iota(jnp.int32, (tk, tq), 1) >> CHUNK_SHIFT
                visible = kk <= qq
            load = lambda: s_scr[slot, g] if visible is None else jnp.where(visible, s_scr[slot, g], NEG_BIG)
            m_new = jnp.maximum(m, jnp.max(load(), axis=0, keepdims=True))
            p = jnp.exp2(load() - m_new).astype(BF16)
            acc = acc * jnp.exp2(m - m_new) + jnp.dot(v_ref[g, kb], p, preferred_element_type=F32)
            out.append((m_new, acc))
        return tuple(out)

    def pair(i, state):
        kb = i * nsub
        for t in range(nsub):
            prefetch(kb + t + 1, (t + 1) % 2)
            state = consume(kb + t, t % 2, state, None)
        return state

    assert nsub == 2
    state = tuple((jnp.full((1, tq), NEG_BIG, F32), jnp.zeros((V_AUG, tq), F32)) for _ in range(ATT_HEADS))
    prefetch(0, 0)
    state = lax.fori_loop(0, qi, pair, state)
    kb = qi * nsub
    prefetch(kb + 1, 1)
    state = consume(kb, 0, state, 0)
    state = consume(kb + 1, 1, state, 1)
    for g in range(ATT_HEADS):
        acc = state[g][1]
        o = acc[:V_HEAD, :] / acc[V_HEAD:V_HEAD + 1, :]
        for a in range(nsub):
            o_ref[g, a] = o[:, a * tk:(a + 1) * tk]


def _attention(q_t, k, v_t):
    heads, nt = q_t.shape[0], q_t.shape[1]
    s = nt * TOK_TILE
    assert s % ATT_TQ == 0 and heads % ATT_HEADS == 0
    qsub = ATT_TQ // TOK_TILE
    resident = dict(pipeline_mode=pl.Buffered(1))
    return pl.pallas_call(
        _attn_kernel,
        grid=(heads // ATT_HEADS, s // ATT_TQ),
        in_specs=[pl.BlockSpec((ATT_HEADS, qsub, QK_PAD, TOK_TILE), lambda h, i: (h, i, 0, 0)),
                  pl.BlockSpec((ATT_HEADS, nt, TOK_TILE, QK_PAD), lambda h, i: (h, 0, 0, 0), **resident),
                  pl.BlockSpec((ATT_HEADS, nt, V_AUG, TOK_TILE), lambda h, i: (h, 0, 0, 0), **resident)],
        out_specs=pl.BlockSpec((ATT_HEADS, qsub, V_HEAD, TOK_TILE), lambda h, i: (h, i, 0, 0)),
        out_shape=jax.ShapeDtypeStruct((heads, nt, V_HEAD, TOK_TILE), F32),
        scratch_shapes=[pltpu.VMEM((2, ATT_HEADS, TOK_TILE, ATT_TQ), F32)],
        compiler_params=_cparams("arbitrary", "arbitrary"),
        name="attention",
    )(q_t, k, v_t)


def _first_lane(cond, lane):
    return jnp.min(jnp.where(cond, lane.astype(F32), 4.0 * LANES), axis=1, keepdims=True).astype(jnp.int32)


def _outproj_router_kernel(x_ref, ya_ref, gb_ref, yb_ref, wo_ref, gf_ref, wrh_ref, wrl_ref, br_ref,
                           h_ref, xn_ref, ri_ref, rf_ref, cnt_ref, base):
    t = TOK_TILE

    @pl.when(pl.program_id(0) == 0)
    def _():
        base[...] = jnp.zeros((1, LANES), F32)

    yb = yb_ref[...].reshape(HEADS * V_HEAD, t).T
    merged = ya_ref[...].astype(F32) + jax.nn.sigmoid(gb_ref[...].astype(F32)) * yb
    h = x_ref[...] + jnp.dot(merged.astype(BF16), wo_ref[...], preferred_element_type=F32)
    h_ref[...] = h
    xn = h * lax.rsqrt(jnp.mean(h * h, axis=-1, keepdims=True) + EPS) * gf_ref[...]
    xn_ref[...] = xn

    x_hi = xn.astype(BF16)
    x_lo = (xn - x_hi.astype(F32)).astype(BF16)
    logits = (jnp.dot(x_hi, wrh_ref[...], preferred_element_type=F32)
              + jnp.dot(x_hi, wrl_ref[...], preferred_element_type=F32)
              + jnp.dot(x_lo, wrh_ref[...], preferred_element_type=F32)) + br_ref[...]

    lane = lax.broadcasted_iota(jnp.int32, (t, LANES), 1)
    is_grp = jnp.logical_and(lane >= N_EXPERTS, lane < N_EXPERTS + N_GROUPS)
    gl = jnp.where(is_grp, logits, NEG_BIG)
    gmax = jnp.max(gl, axis=1, keepdims=True)
    grp = _first_lane(gl == gmax, lane) - N_EXPERTS
    p_grp = 1.0 / jnp.sum(jnp.where(is_grp, jnp.exp(gl - gmax), 0.0), axis=1, keepdims=True)
    el = jnp.where((lane >> GROUP_SHIFT) == grp, logits, NEG_BIG)
    v1 = jnp.max(el, axis=1, keepdims=True)
    e1 = _first_lane(el == v1, lane)
    el2 = jnp.where(lane == e1, NEG_BIG, el)
    v2 = jnp.max(el2, axis=1, keepdims=True)
    e2 = _first_lane(el2 == v2, lane)
    ex = jnp.exp(v2 - v1)
    w1 = p_grp / (1.0 + ex)
    w2 = p_grp * ex / (1.0 + ex)

    tri = (lax.broadcasted_iota(jnp.int32, (t, t), 1) < lax.broadcasted_iota(jnp.int32, (t, t), 0))
    tri = jnp.where(tri, 1.0, 0.0).astype(BF16)
    oh1 = jnp.where(lane == e1, 1.0, 0.0)
    oh2 = jnp.where(lane == e2, 1.0, 0.0)
    pre1 = jnp.dot(tri, oh1.astype(BF16), preferred_element_type=F32) + base[...]
    base1 = base[...] + jnp.sum(oh1, axis=0, keepdims=True)
    pre2 = jnp.dot(tri, oh2.astype(BF16), preferred_element_type=F32) + base1
    base2 = base1 + jnp.sum(oh2, axis=0, keepdims=True)
    r1 = jnp.sum(oh1 * pre1, axis=1, keepdims=True).astype(jnp.int32)
    r2 = jnp.sum(oh2 * pre2, axis=1, keepdims=True).astype(jnp.int32)
    base[...] = base2
    cnt_ref[...] = base2

    ri_ref[...] = jnp.where(lane == 0, e1, jnp.where(lane == 1, e2, jnp.where(lane == 2, r1, r2)))
    rf_ref[...] = jnp.where(lane == 0, w1, w2)


def _outproj_router(x2, ya_g, gl, yb_t, w_out, gf, wr_hi, wr_lo, br):
    s = x2.shape[0]
    t = TOK_TILE
    row = lambda n: pl.BlockSpec((t, n), lambda i: (i, 0))
    return pl.pallas_call(
        _outproj_router_kernel,
        grid=(s // t,),
        in_specs=[row(D_MODEL), row(D_MODEL), pl.BlockSpec((t, D_MODEL), lambda i: (i, 1)),
                  pl.BlockSpec((HEADS, None, V_HEAD, t), lambda i: (0, i, 0, 0)),
                  _const_spec((D_MODEL, D_MODEL)), _const_spec((1, D_MODEL)),
                  _const_spec((D_MODEL, LANES)), _const_spec((D_MODEL, LANES)), _const_spec((1, LANES))],
        out_specs=[row(D_MODEL), row(D_MODEL), row(LANES), row(LANES), _const_spec((1, LANES))],
        out_shape=[jax.ShapeDtypeStruct((s, D_MODEL), F32), jax.ShapeDtypeStruct((s, D_MODEL), F32),
                   jax.ShapeDtypeStruct((s, LANES), jnp.int32), jax.ShapeDtypeStruct((s, LANES), F32),
                   jax.ShapeDtypeStruct((1, LANES), F32)],
        scratch_shapes=[pltpu.VMEM((1, LANES), F32)],
        compiler_params=_cparams("arbitrary"),
        name="outproj_router",
    )(x2, ya_g, gl, yb_t, w_out, gf, wr_hi, wr_lo, br)


def _row_copy(src_ref, src_row, dst_ref, dst_row, sem):
    return pltpu.make_async_copy(src_ref.at[pl.ds(src_row, 1), :], dst_ref.at[pl.ds(dst_row, 1), :], sem)


def _scatter_kernel(d1_ref, d2_ref, xn_ref, zeros_ref, xpad_ref, sem):
    del zeros_ref
    t = TOK_TILE

    def issue(i, c):
        _row_copy(xn_ref, i, xpad_ref, d1_ref[0, i], sem).start()
        _row_copy(xn_ref, i, xpad_ref, d2_ref[0, i], sem).start()
        return c

    lax.fori_loop(0, t, issue, 0)

    def drain(i, c):
        _row_copy(xn_ref, 0, xpad_ref, 0, sem).wait()
        _row_copy(xn_ref, 0, xpad_ref, 0, sem).wait()
        return c

    lax.fori_loop(0, t, drain, 0)


def _scatter_rows(d1, d2, xn, xpad_zeros):
    s = xn.shape[0]
    t = TOK_TILE
    idx = pl.BlockSpec((None, 1, t), lambda i: (i, 0, 0), memory_space=pltpu.SMEM)
    return pl.pallas_call(
        _scatter_kernel,
        grid=(s // t,),
        in_specs=[idx, idx, pl.BlockSpec((t, D_MODEL), lambda i: (i, 0)), pl.BlockSpec(memory_space=pl.ANY)],
        out_specs=pl.BlockSpec(memory_space=pl.ANY),
        out_shape=jax.ShapeDtypeStruct(xpad_zeros.shape, F32),
        scratch_shapes=[pltpu.SemaphoreType.DMA(())],
        input_output_aliases={3: 0},
        compiler_params=_cparams("arbitrary"),
        name="moe_scatter",
    )(d1, d2, xn, xpad_zeros)


def _expert_kernel(be_ref, na_ref, x_ref, wgu_ref, wd_ref, y_ref):
    del be_ref
    active = pl.program_id(0) < na_ref[0]

    @pl.when(active)
    def _():
        x = x_ref[...].astype(BF16)
        gu = jnp.dot(x, wgu_ref[...], preferred_element_type=F32)
        g, u = gu[:, :D_EXPERT], gu[:, D_EXPERT:]
        hid = (g * jax.nn.sigmoid(g) * u).astype(BF16)
        y_ref[...] = jnp.dot(hid, wd_ref[...], preferred_element_type=F32)

    @pl.when(jnp.logical_not(active))
    def _():
        y_ref[...] = jnp.zeros_like(y_ref)


def _experts(blk_expert, n_active, xpad, w_gu, w_down):
    nblk = xpad.shape[0] // MOE_BLOCK
    rows = lambda b, be, na: (jnp.minimum(b, na[0] - 1), 0)
    grid_spec = pltpu.PrefetchScalarGridSpec(
        num_scalar_prefetch=2,
        grid=(nblk,),
        in_specs=[pl.BlockSpec((MOE_BLOCK, D_MODEL), rows),
                  pl.BlockSpec((None, D_MODEL, 2 * D_EXPERT), lambda b, be, na: (be[b], 0, 0)),
                  pl.BlockSpec((None, D_EXPERT, D_MODEL), lambda b, be, na: (be[b], 0, 0))],
        out_specs=pl.BlockSpec((MOE_BLOCK, D_MODEL), lambda b, be, na: (b, 0)),
    )
    return pl.pallas_call(
        _expert_kernel,
        grid_spec=grid_spec,
        out_shape=jax.ShapeDtypeStruct(xpad.shape, F32),
        compiler_params=_cparams("arbitrary"),
        name="moe_experts",
    )(blk_expert, n_active, xpad, w_gu, w_down)


def _combine_kernel(d1_ref, d2_ref, h_ref, rf_ref, ypad_ref, out_ref, ybuf, sem):
    t = TOK_TILE

    def issue(i, c):
        _row_copy(ypad_ref, d1_ref[0, i], ybuf.at[0], i, sem).start()
        _row_copy(ypad_ref, d2_ref[0, i], ybuf.at[1], i, sem).start()
        return c

    lax.fori_loop(0, t, issue, 0)

    def drain(i, c):
        _row_copy(ypad_ref, 0, ybuf.at[0], 0, sem).wait()
        _row_copy(ypad_ref, 0, ybuf.at[1], 0, sem).wait()
        return c

    lax.fori_loop(0, t, drain, 0)
    rf = rf_ref[...]
    out_ref[...] = h_ref[...] + rf[:, 0:1] * ybuf[0] + rf[:, 1:2] * ybuf[1]


def _combine(d1, d2, h, rf, ypad):
    s = h.shape[0]
    t = TOK_TILE
    idx = pl.BlockSpec((None, 1, t), lambda i: (i, 0, 0), memory_space=pltpu.SMEM)
    return pl.pallas_call(
        _combine_kernel,
        grid=(s // t,),
        in_specs=[idx, idx, pl.BlockSpec((t, D_MODEL), lambda i: (i, 0)),
                  pl.BlockSpec((t, LANES), lambda i: (i, 0)), pl.BlockSpec(memory_space=pl.ANY)],
        out_specs=pl.BlockSpec((t, D_MODEL), lambda i: (i, 0)),
        out_shape=jax.ShapeDtypeStruct((s, D_MODEL), F32),
        scratch_shapes=[pltpu.VMEM((2, t, D_MODEL), F32), pltpu.SemaphoreType.DMA(())],
        compiler_params=_cparams("arbitrary"),
        name="moe_combine",
    )(d1, d2, h, rf, ypad)


def _layer(h3, positions, norm_mix_g, w_in, conv_w, conv_b, lru_wa, lru_ba, lru_wx, lru_bx, lru_lambda,
           q_a_g, w_uq, kv_a_g, w_ukv, q_norm_g, k_norm_g, w_out, norm_ffn_g,
           router_group_w, router_group_b, router_expert_w, router_expert_b, w_gate, w_up, w_down):
    b, s, _ = h3.shape
    assert b == 1 and s % TOK_TILE == 0
    x2 = h3.reshape(s, D_MODEL)
    nt = s // TOK_TILE

    o = np.cumsum((0, D_RNN, D_RNN, Q_LORA, KV_LORA, QK_ROPE))
    w_cat = jnp.concatenate(
        [w_in[:, o[0]:o[4]], w_in[:, o[4]:o[5]], jnp.zeros((D_MODEL, LANES - QK_ROPE), F32), w_in[:, o[5]:]],
        axis=1).astype(BF16)
    row = lambda v: v.reshape(1, -1)
    col = lambda v: v.reshape(-1, 1)
    wq_t = w_uq.T.astype(BF16)
    ukv = w_ukv.reshape(KV_LORA, HEADS, QK_NOPE + V_HEAD)
    wk_t = ukv[:, :, :QK_NOPE].reshape(KV_LORA, HEADS * QK_NOPE).T.astype(BF16)
    wv_t = ukv[:, :, QK_NOPE:].reshape(KV_LORA, HEADS * V_HEAD).T.astype(BF16)
    half = QK_ROPE // 2
    freq = col(ROPE_THETA ** (-jnp.arange(half, dtype=F32) / half))
    wr = jnp.concatenate([router_expert_w, router_group_w,
                          jnp.zeros((D_MODEL, LANES - N_EXPERTS - N_GROUPS), F32)], axis=1)
    wr_hi = wr.astype(BF16)
    wr_lo = (wr - wr_hi.astype(F32)).astype(BF16)
    br = row(jnp.concatenate([router_expert_b, router_group_b, jnp.zeros((LANES - N_EXPERTS - N_GROUPS,), F32)]))
    w_gu = jnp.concatenate([w_gate, w_up], axis=2).astype(BF16)

    x_rnn, g_rnn, q_c, kv_c, pe, gl = _inproj(x2, row(norm_mix_g), w_cat)
    ya_g = _lru(x_rnn, g_rnn, gl, conv_w, row(conv_b), lru_wa.astype(BF16), row(lru_ba),
                lru_wx.astype(BF16), row(lru_bx), row(lru_lambda))
    q_t, k_r, v_t = _mla_prep(q_c, kv_c, pe, positions.reshape(nt, 1, TOK_TILE), freq, col(q_a_g), col(kv_a_g),
                              col(q_norm_g), col(k_norm_g), wq_t, wk_t, wv_t)
    yb_t = _attention(q_t, k_r, v_t)
    h, xn, ri, rf, counts = _outproj_router(x2, ya_g, gl, yb_t, w_out.astype(BF16), row(norm_ffn_g),
                                            wr_hi, wr_lo, br)

    cnt = counts[0, :N_EXPERTS].astype(jnp.int32)
    padded = (cnt + MOE_BLOCK - 1) // MOE_BLOCK * MOE_BLOCK
    p_end = jnp.cumsum(padded)
    p_start = p_end - padded
    n_slots = 2 * s + N_EXPERTS * MOE_BLOCK
    nblk = n_slots // MOE_BLOCK
    blk_first = jnp.arange(nblk, dtype=jnp.int32) * MOE_BLOCK
    blk_expert = jnp.minimum(jnp.sum((p_end[None, :] <= blk_first[:, None]).astype(jnp.int32), axis=1),
                             N_EXPERTS - 1)
    n_active = (p_end[-1:] // MOE_BLOCK).astype(jnp.int32)
    d1 = (p_start[ri[:, 0]] + ri[:, 2]).reshape(nt, 1, TOK_TILE)
    d2 = (p_start[ri[:, 1]] + ri[:, 3]).reshape(nt, 1, TOK_TILE)

    xpad = _scatter_rows(d1, d2, xn, jnp.zeros((n_slots, D_MODEL), F32))
    ypad = _experts(blk_expert, n_active, xpad, w_gu, w_down.astype(BF16))
    out = _combine(d1, d2, h, rf, ypad)
    return out.reshape(b, s, D_MODEL)


def kernel(x, positions, norm_mix_g, w_in, conv_w, conv_b, lru_wa, lru_ba, lru_wx, lru_bx, lru_lambda, q_a_g,
           w_uq, kv_a_g, w_ukv, q_norm_g, k_norm_g, w_out, norm_ffn_g, router_group_w, router_group_b,
           router_expert_w, router_expert_b, w_gate, w_up, w_down):
    h = x
    for l in range(norm_mix_g.shape[0]):
        h = _layer(h, positions, norm_mix_g[l], w_in[l], conv_w[l], conv_b[l], lru_wa[l], lru_ba[l], lru_wx[l],
                   lru_bx[l], lru_lambda[l], q_a_g[l], w_uq[l], kv_a_g[l], w_ukv[l], q_norm_g[l], k_norm_g[l],
                   w_out[l], norm_ffn_g[l], router_group_w[l], router_group_b[l], router_expert_w[l],
                   router_expert_b[l], w_gate[l], w_up[l], w_down[l])
    return h
```

```python
import functools

import jax
import jax.numpy as jnp
import numpy as np
from jax import lax
from jax.experimental import pallas as pl
from jax.experimental.pallas import tpu as pltpu

F32 = jnp.float32
BF16 = jnp.bfloat16

D_MODEL = 1024
EPS = 1e-6
CHUNK = 64
D_RNN = 1024
LRU_BLOCKS = 4
LRU_BLOCK_W = 256
CONV_W = 4
LRU_C = 8.0
HEADS = 16
V_HEAD = 64
QK_NOPE = 64
QK_ROPE = 32
QK_HEAD = 96
Q_LORA = 768
KV_LORA = 256
ROPE_THETA = 10000.0
N_GROUPS = 4
EXPERTS_PER_GROUP = 8
N_EXPERTS = 32
D_EXPERT = 256
CHUNK_SHIFT = CHUNK.bit_length() - 1
GROUP_SHIFT = EXPERTS_PER_GROUP.bit_length() - 1

LANES = 128
TOK_TILE = 256
ATT_TQ = 512
ATT_HEADS = 4
ATT_FIXED_REF_LIMIT = 48.0
QK_PAD = 128
V_AUG = 80
MOE_BLOCK = 256
DMA_UNROLL = 8
NEG_BIG = -1e30
LOG2E = 1.4426950408889634
VMEM_LIMIT = 56 * 1024 * 1024

_SEG_XR, _SEG_GR, _SEG_QC, _SEG_KV, _SEG_PE, _SEG_GL = 0, 1024, 2048, 2816, 3072, 3200
_N_IN = 5248


def _cparams(*sem):
    return pltpu.CompilerParams(dimension_semantics=sem, vmem_limit_bytes=VMEM_LIMIT)


def _const_spec(shape):
    nd = len(shape)
    return pl.BlockSpec(shape, lambda *_: (0,) * nd)


def _inproj_kernel(x_ref, g_ref, w_ref, xr_ref, gr_ref, qc_ref, kv_ref, pe_ref, gl_ref):
    x = x_ref[...]
    ms = jnp.mean(x * x, axis=-1, keepdims=True)
    u = (x * lax.rsqrt(ms + EPS) * g_ref[...]).astype(BF16)
    p = jnp.dot(u, w_ref[...], preferred_element_type=F32)
    xr_ref[...] = p[:, _SEG_XR:_SEG_GR].astype(BF16)
    gr_ref[...] = p[:, _SEG_GR:_SEG_QC].astype(BF16)
    qc_ref[...] = p[:, _SEG_QC:_SEG_KV].astype(BF16)
    kv_ref[...] = p[:, _SEG_KV:_SEG_PE].astype(BF16)
    pe_ref[...] = p[:, _SEG_PE:_SEG_GL]
    gl_ref[...] = p[:, _SEG_GL:_N_IN].astype(BF16)


def _inproj(x2, g, w_cat):
    s = x2.shape[0]
    t = TOK_TILE
    row = lambda n: pl.BlockSpec((t, n), lambda i: (i, 0))
    return pl.pallas_call(
        _inproj_kernel,
        grid=(s // t,),
        in_specs=[row(D_MODEL), _const_spec((1, D_MODEL)), _const_spec((D_MODEL, _N_IN))],
        out_specs=[row(1024), row(1024), row(Q_LORA), row(KV_LORA), row(LANES), row(2048)],
        out_shape=[
            jax.ShapeDtypeStruct((s, 1024), BF16), jax.ShapeDtypeStruct((s, 1024), BF16),
            jax.ShapeDtypeStruct((s, Q_LORA), BF16), jax.ShapeDtypeStruct((s, KV_LORA), BF16),
            jax.ShapeDtypeStruct((s, LANES), F32), jax.ShapeDtypeStruct((s, 2048), BF16),
        ],
        compiler_params=_cparams("arbitrary"),
        name="inproj",
    )(x2, g, w_cat)


def _lru_kernel(xr_ref, gr_ref, ga_ref, cw_ref, cb_ref, wa_ref, ba_ref, wx_ref, bx_ref, lam_ref,
                out_ref, xbuf, hprev):
    t = TOK_TILE

    @pl.when(pl.program_id(0) == 0)
    def _():
        xbuf[0:8, :] = jnp.zeros((8, D_RNN), F32)
        hprev[...] = jnp.zeros((1, D_RNN), F32)

    x = xr_ref[...].astype(F32)
    xbuf[8:8 + t, :] = x
    xa = cb_ref[...] + cw_ref[3:4, :] * x
    for j in range(CONV_W - 1):
        xa = xa + cw_ref[j:j + 1, :] * xbuf[5 + j:5 + j + t, :]
    xbuf[0:8, :] = x[t - 8:t, :]

    r_parts, i_parts = [], []
    for b in range(LRU_BLOCKS):
        xb = xa[:, b * LRU_BLOCK_W:(b + 1) * LRU_BLOCK_W].astype(BF16)
        r_parts.append(jnp.dot(xb, wa_ref[b], preferred_element_type=F32))
        i_parts.append(jnp.dot(xb, wx_ref[b], preferred_element_type=F32))
    r = jax.nn.sigmoid(jnp.concatenate(r_parts, axis=1) + ba_ref[...])
    gi = jax.nn.sigmoid(jnp.concatenate(i_parts, axis=1) + bx_ref[...])
    z = -lam_ref[...]
    softplus = jnp.maximum(z, 0.0) + jnp.log(1.0 + jnp.exp(-jnp.abs(z)))
    log_a = (-LRU_C) * r * softplus
    a = jnp.exp(log_a)
    b_in = jnp.sqrt(1.0 - a * a) * (gi * xa)

    row = lax.broadcasted_iota(jnp.int32, (t, D_RNN), 0)
    d = 1
    while d < t:
        valid = row >= d
        a_sh = pltpu.roll(a, d, axis=0)
        b_sh = pltpu.roll(b_in, d, axis=0)
        b_in = jnp.where(valid, a * b_sh, 0.0) + b_in
        a = jnp.where(valid, a * a_sh, a)
        d *= 2
    h = a * hprev[...] + b_in
    hprev[...] = h[t - 1:t, :]

    ya = h * jax.nn.gelu(gr_ref[...].astype(F32))
    out_ref[...] = (jax.nn.sigmoid(ga_ref[...].astype(F32)) * ya).astype(BF16)


def _lru(x_rnn, g_rnn, gl, conv_w, conv_b, wa, ba, wx, bx, lam):
    s = x_rnn.shape[0]
    t = TOK_TILE
    row = pl.BlockSpec((t, 1024), lambda i: (i, 0))
    return pl.pallas_call(
        _lru_kernel,
        grid=(s // t,),
        in_specs=[row, row, row, _const_spec((CONV_W, D_RNN)), _const_spec((1, D_RNN)),
                  _const_spec((LRU_BLOCKS, LRU_BLOCK_W, LRU_BLOCK_W)), _const_spec((1, D_RNN)),
                  _const_spec((LRU_BLOCKS, LRU_BLOCK_W, LRU_BLOCK_W)), _const_spec((1, D_RNN)),
                  _const_spec((1, D_RNN))],
        out_specs=row,
        out_shape=jax.ShapeDtypeStruct((s, D_RNN), BF16),
        scratch_shapes=[pltpu.VMEM((8 + t, D_RNN), F32), pltpu.VMEM((1, D_RNN), F32)],
        compiler_params=_cparams("arbitrary"),
        name="lru",
    )(x_rnn, g_rnn, gl, conv_w, conv_b, wa, ba, wx, bx, lam)


def _rope_rows(x, cos, sin):
    half = QK_ROPE // 2
    x1, x2 = x[:, :half, :], x[:, half:, :]
    return x1 * cos - x2 * sin, x1 * sin + x2 * cos


def _mla_prep_kernel(qc_ref, kv_ref, pe_ref, pos_ref, freq_ref, gq_ref, gkv_ref, qg_ref, kg_ref,
                     wq_ref, wk_ref, wv_ref, q_out, k_out, v_out, qn_out, kn_out):
    t = TOK_TILE
    ang = pos_ref[...].astype(F32) * freq_ref[...]
    cos, sin = jnp.cos(ang), jnp.sin(ang)

    qct = qc_ref[...].astype(F32).T
    qcn = qct * lax.rsqrt(jnp.mean(qct * qct, axis=0, keepdims=True) + EPS) * gq_ref[...]
    q = jnp.dot(wq_ref[...], qcn.astype(BF16), preferred_element_type=F32)
    q = q.reshape(HEADS, QK_HEAD, t)
    q = q * lax.rsqrt(jnp.mean(q * q, axis=1, keepdims=True) + EPS) * qg_ref[...][None]
    r1, r2 = _rope_rows(q[:, QK_NOPE:, :], cos, sin)
    scale = (QK_HEAD ** -0.5) * LOG2E
    q_bf = (jnp.concatenate([q[:, :QK_NOPE, :], r1, r2], axis=1) * scale).astype(BF16)
    q_out[...] = jnp.concatenate([q_bf, jnp.zeros((HEADS, QK_PAD - QK_HEAD, t), BF16)], axis=1)
    q_f = q_bf.astype(F32)
    qn_out[...] = jnp.sqrt(jnp.sum(q_f * q_f, axis=1, keepdims=True))

    kvt = kv_ref[...].astype(F32).T
    kvn = (kvt * lax.rsqrt(jnp.mean(kvt * kvt, axis=0, keepdims=True) + EPS) * gkv_ref[...]).astype(BF16)
    kn = jnp.dot(wk_ref[...], kvn, preferred_element_type=F32).reshape(HEADS, QK_NOPE, t)
    v = jnp.dot(wv_ref[...], kvn, preferred_element_type=F32).reshape(HEADS, V_HEAD, t)
    pe = pe_ref[...].T[:QK_ROPE, :]
    ssq = jnp.sum(kn * kn, axis=1, keepdims=True) + jnp.sum(pe * pe, axis=0, keepdims=True)[None]
    rstd = lax.rsqrt(ssq * (1.0 / QK_HEAD) + EPS)
    kg = kg_ref[...]
    k_nope = kn * rstd * kg[None, :QK_NOPE, :]
    k_pe = (pe * kg[QK_NOPE:, :])[None] * rstd
    p1, p2 = _rope_rows(k_pe, cos, sin)
    k_real = jnp.concatenate([k_nope, p1, p2], axis=1)
    pad = jnp.where(lax.broadcasted_iota(jnp.int32, (HEADS, QK_PAD - QK_HEAD, t), 1) == 0, 1.0, 0.0)
    kt = jnp.concatenate([k_real, pad], axis=1)
    for h in range(HEADS):
        k_out[h] = kt[h].T.astype(BF16)
    k_f = k_real.astype(BF16).astype(F32)
    k_max = jnp.max(jnp.sum(k_f * k_f, axis=1, keepdims=True), axis=2, keepdims=True)
    kn_out[...] = jnp.broadcast_to(jnp.sqrt(k_max), (HEADS, 1, LANES))
    v_out[...] = jnp.concatenate([v, jnp.ones((HEADS, V_AUG - V_HEAD, t), F32)], axis=1).astype(BF16)


def _mla_prep(q_c, kv_c, pe, pos_row, freq, gq, gkv, qg, kg, wq_t, wk_t, wv_t):
    s = q_c.shape[0]
    t = TOK_TILE
    nt = s // t
    row = lambda n: pl.BlockSpec((t, n), lambda i: (i, 0))
    blk = lambda r, c: pl.BlockSpec((HEADS, None, r, c), lambda i: (0, i, 0, 0))
    return pl.pallas_call(
        _mla_prep_kernel,
        grid=(nt,),
        in_specs=[row(Q_LORA), row(KV_LORA), row(LANES),
                  pl.BlockSpec((None, 1, t), lambda i: (i, 0, 0)),
                  _const_spec((QK_ROPE // 2, 1)), _const_spec((Q_LORA, 1)), _const_spec((KV_LORA, 1)),
                  _const_spec((QK_HEAD, 1)), _const_spec((QK_HEAD, 1)),
                  _const_spec((HEADS * QK_HEAD, Q_LORA)), _const_spec((HEADS * QK_NOPE, KV_LORA)),
                  _const_spec((HEADS * V_HEAD, KV_LORA))],
        out_specs=[blk(QK_PAD, t), blk(t, QK_PAD), blk(V_AUG, t), blk(1, t), blk(1, LANES)],
        out_shape=[jax.ShapeDtypeStruct((HEADS, nt, QK_PAD, t), BF16),
                   jax.ShapeDtypeStruct((HEADS, nt, t, QK_PAD), BF16),
                   jax.ShapeDtypeStruct((HEADS, nt, V_AUG, t), BF16),
                   jax.ShapeDtypeStruct((HEADS, nt, 1, t), F32),
                   jax.ShapeDtypeStruct((HEADS, nt, 1, LANES), F32)],
        compiler_params=_cparams("arbitrary"),
        name="mla_prep",
    )(q_c, kv_c, pe, pos_row, freq, gq, gkv, qg, kg, wq_t, wk_t, wv_t)


def _attn_kernel(fixed_ref, *refs):
    if fixed_ref:
        q_ref, k_ref, v_ref, qn_ref, kn_ref, o_ref, s_scr = refs
    else:
        q_ref, k_ref, v_ref, o_ref, s_scr = refs
    tq, tk = ATT_TQ, TOK_TILE
    nsub = tq // tk
    qi = pl.program_id(1)

    def query_block(g, a):
        q = q_ref[g, a]
        if not fixed_ref:
            return q
        k_max = jnp.max(kn_ref[g], axis=0)[:, :1]
        ref = qn_ref[g, a] * k_max
        row = lax.broadcasted_iota(jnp.int32, (QK_PAD - QK_HEAD, tk), 0)
        extra = jnp.where(row == 0, -ref, 0.0).astype(BF16)
        return jnp.concatenate([q[:QK_HEAD, :], extra], axis=0)

    qs = [jnp.concatenate([query_block(g, a) for a in range(nsub)], axis=1) for g in range(ATT_HEADS)]

    def prefetch(kb, slot):
        for g in range(ATT_HEADS):
            s_scr[slot, g] = jnp.dot(k_ref[g, kb], qs[g], preferred_element_type=F32)

    def consume(kb, slot, state, tail):
        out = []
        for g in range(ATT_HEADS):
            m, acc = state[g]
            if tail is None:
                visible = None
            else:
                kk = (lax.broadcasted_iota(jnp.int32, (tk, tq), 0) + tail * tk) >> CHUNK_SHIFT
                qq = lax.broadcasted_iota(jnp.int32, (tk, tq), 1) >> CHUNK_SHIFT
                visible = kk <= qq
            load = lambda: s_scr[slot, g] if visible is None else jnp.where(visible, s_scr[slot, g], NEG_BIG)
            if fixed_ref:
                p = jnp.exp2(load()).astype(BF16)
                acc = acc + jnp.dot(v_ref[g, kb], p, preferred_element_type=F32)
                out.append((m, acc))
            else:
                m_new = jnp.maximum(m, jnp.max(load(), axis=0, keepdims=True))
                p = jnp.exp2(load() - m_new).astype(BF16)
                acc = acc * jnp.exp2(m - m_new) + jnp.dot(v_ref[g, kb], p, preferred_element_type=F32)
                out.append((m_new, acc))
        return tuple(out)

    def pair(i, state):
        for t in range(2):
            prefetch(2 * i + t + 1, (t + 1) % 2)
            state = consume(2 * i + t, t, state, None)
        return state

    assert nsub % 2 == 0
    state = tuple((jnp.full((1, tq), NEG_BIG, F32), jnp.zeros((V_AUG, tq), F32)) for _ in range(ATT_HEADS))
    prefetch(0, 0)
    state = lax.fori_loop(0, qi * (nsub // 2), pair, state)
    kb = qi * nsub
    for t in range(nsub):
        if t + 1 < nsub:
            prefetch(kb + t + 1, (t + 1) % 2)
        state = consume(kb + t, t % 2, state, t)
    for g in range(ATT_HEADS):
        acc = state[g][1]
        o = acc[:V_HEAD, :] / acc[V_HEAD:V_HEAD + 1, :]
        for a in range(nsub):
            o_ref[g, a] = o[:, a * tk:(a + 1) * tk]


def _attention(q_t, k, v_t, q_norm, k_norm, fixed_ref):
    heads, nt = q_t.shape[0], q_t.shape[1]
    s = nt * TOK_TILE
    assert s % ATT_TQ == 0 and heads % ATT_HEADS == 0
    qsub = ATT_TQ // TOK_TILE
    resident = dict(pipeline_mode=pl.Buffered(1))
    in_specs = [pl.BlockSpec((ATT_HEADS, qsub, QK_PAD, TOK_TILE), lambda h, i: (h, i, 0, 0)),
                pl.BlockSpec((ATT_HEADS, nt, TOK_TILE, QK_PAD), lambda h, i: (h, 0, 0, 0), **resident),
                pl.BlockSpec((ATT_HEADS, nt, V_AUG, TOK_TILE), lambda h, i: (h, 0, 0, 0), **resident)]
    args = (q_t, k, v_t)
    if fixed_ref:
        in_specs += [pl.BlockSpec((ATT_HEADS, qsub, 1, TOK_TILE), lambda h, i: (h, i, 0, 0)),
                     pl.BlockSpec((ATT_HEADS, nt, 1, LANES), lambda h, i: (h, 0, 0, 0))]
        args += (q_norm, k_norm)
    return pl.pallas_call(
        functools.partial(_attn_kernel, fixed_ref),
        grid=(heads // ATT_HEADS, s // ATT_TQ),
        in_specs=in_specs,
        out_specs=pl.BlockSpec((ATT_HEADS, qsub, V_HEAD, TOK_TILE), lambda h, i: (h, i, 0, 0)),
        out_shape=jax.ShapeDtypeStruct((heads, nt, V_HEAD, TOK_TILE), F32),
        scratch_shapes=[pltpu.VMEM((2, ATT_HEADS, TOK_TILE, ATT_TQ), F32)],
        compiler_params=_cparams("arbitrary", "arbitrary"),
        name="attention_fixed_ref" if fixed_ref else "attention_online",
    )(*args)


def _first_lane(cond, lane):
    return jnp.min(jnp.where(cond, lane.astype(F32), 4.0 * LANES), axis=1, keepdims=True).astype(jnp.int32)


def _outproj_router_kernel(x_ref, ya_ref, gb_ref, yb_ref, wo_ref, gf_ref, wrh_ref, wrl_ref, br_ref,
                           h_ref, xn_ref, ri_ref, rf_ref, cnt_ref, base):
    t = TOK_TILE

    @pl.when(pl.program_id(0) == 0)
    def _():
        base[...] = jnp.zeros((1, LANES), F32)

    yb = yb_ref[...].reshape(HEADS * V_HEAD, t).T
    merged = ya_ref[...].astype(F32) + jax.nn.sigmoid(gb_ref[...].astype(F32)) * yb
    h = x_ref[...] + jnp.dot(merged.astype(BF16), wo_ref[...], preferred_element_type=F32)
    h_ref[...] = h
    xn = h * lax.rsqrt(jnp.mean(h * h, axis=-1, keepdims=True) + EPS) * gf_ref[...]
    xn_ref[...] = xn

    x_hi = xn.astype(BF16)
    x_lo = (xn - x_hi.astype(F32)).astype(BF16)
    logits = (jnp.dot(x_hi, wrh_ref[...], preferred_element_type=F32)
              + jnp.dot(x_hi, wrl_ref[...], preferred_element_type=F32)
              + jnp.dot(x_lo, wrh_ref[...], preferred_element_type=F32)) + br_ref[...]

    lane = lax.broadcasted_iota(jnp.int32, (t, LANES), 1)
    is_grp = jnp.logical_and(lane >= N_EXPERTS, lane < N_EXPERTS + N_GROUPS)
    gl = jnp.where(is_grp, logits, NEG_BIG)
    gmax = jnp.max(gl, axis=1, keepdims=True)
    grp = _first_lane(gl == gmax, lane) - N_EXPERTS
    p_grp = 1.0 / jnp.sum(jnp.where(is_grp, jnp.exp(gl - gmax), 0.0), axis=1, keepdims=True)
    el = jnp.where((lane >> GROUP_SHIFT) == grp, logits, NEG_BIG)
    v1 = jnp.max(el, axis=1, keepdims=True)
    e1 = _first_lane(el == v1, lane)
    el2 = jnp.where(lane == e1, NEG_BIG, el)
    v2 = jnp.max(el2, axis=1, keepdims=True)
    e2 = _first_lane(el2 == v2, lane)
    ex = jnp.exp(v2 - v1)
    w1 = p_grp / (1.0 + ex)
    w2 = p_grp * ex / (1.0 + ex)

    tri = (lax.broadcasted_iota(jnp.int32, (t, t), 1) < lax.broadcasted_iota(jnp.int32, (t, t), 0))
    tri = jnp.where(tri, 1.0, 0.0).astype(BF16)
    oh1 = jnp.where(lane == e1, 1.0, 0.0)
    oh2 = jnp.where(lane == e2, 1.0, 0.0)
    pre1 = jnp.dot(tri, oh1.astype(BF16), preferred_element_type=F32) + base[...]
    base1 = base[...] + jnp.sum(oh1, axis=0, keepdims=True)
    pre2 = jnp.dot(tri, oh2.astype(BF16), preferred_element_type=F32) + base1
    base2 = base1 + jnp.sum(oh2, axis=0, keepdims=True)
    r1 = jnp.sum(oh1 * pre1, axis=1, keepdims=True).astype(jnp.int32)
    r2 = jnp.sum(oh2 * pre2, axis=1, keepdims=True).astype(jnp.int32)
    base[...] = base2
    cnt_ref[...] = base2

    ri = jnp.where(lane == 0, e1, jnp.where(lane == 1, e2, jnp.where(lane == 2, r1, r2)))
    ri_ref[...] = ri.astype(F32).T[:8, :].astype(jnp.int32)
    rf_ref[...] = jnp.where(lane == 0, w1, w2)


def _outproj_router(x2, ya_g, gl, yb_t, w_out, gf, wr_hi, wr_lo, br):
    s = x2.shape[0]
    t = TOK_TILE
    row = lambda n: pl.BlockSpec((t, n), lambda i: (i, 0))
    return pl.pallas_call(
        _outproj_router_kernel,
        grid=(s // t,),
        in_specs=[row(D_MODEL), row(D_MODEL), pl.BlockSpec((t, D_MODEL), lambda i: (i, 1)),
                  pl.BlockSpec((HEADS, None, V_HEAD, t), lambda i: (0, i, 0, 0)),
                  _const_spec((D_MODEL, D_MODEL)), _const_spec((1, D_MODEL)),
                  _const_spec((D_MODEL, LANES)), _const_spec((D_MODEL, LANES)), _const_spec((1, LANES))],
        out_specs=[row(D_MODEL), row(D_MODEL), pl.BlockSpec((None, 8, t), lambda i: (i, 0, 0)), row(LANES),
                   _const_spec((1, LANES))],
        out_shape=[jax.ShapeDtypeStruct((s, D_MODEL), F32), jax.ShapeDtypeStruct((s, D_MODEL), F32),
                   jax.ShapeDtypeStruct((s // t, 8, t), jnp.int32), jax.ShapeDtypeStruct((s, LANES), F32),
                   jax.ShapeDtypeStruct((1, LANES), F32)],
        scratch_shapes=[pltpu.VMEM((1, LANES), F32)],
        compiler_params=_cparams("arbitrary"),
        name="outproj_router",
    )(x2, ya_g, gl, yb_t, w_out, gf, wr_hi, wr_lo, br)


def _row_copy(src_ref, src_row, dst_ref, dst_row, sem):
    return pltpu.make_async_copy(src_ref.at[pl.ds(src_row, 1), :], dst_ref.at[pl.ds(dst_row, 1), :], sem)


def _slot(ps_ref, ri_ref, k, i):
    return ps_ref[ri_ref[k, i]] + ri_ref[2 + k, i]


def _scatter_kernel(ps_ref, ri_ref, xn_ref, zeros_ref, xpad_ref, sem):
    del zeros_ref
    t = TOK_TILE

    def issue(i, c):
        for k in range(2):
            _row_copy(xn_ref, i, xpad_ref, _slot(ps_ref, ri_ref, k, i), sem).start(priority=k)
        return c

    lax.fori_loop(0, t, issue, 0, unroll=DMA_UNROLL)

    def drain(i, c):
        for _ in range(2):
            _row_copy(xn_ref, 0, xpad_ref, 0, sem).wait()
        return c

    lax.fori_loop(0, t, drain, 0, unroll=DMA_UNROLL)


def _scatter_rows(p_start, ri_t, xn, xpad_zeros):
    s = xn.shape[0]
    t = TOK_TILE
    grid_spec = pltpu.PrefetchScalarGridSpec(
        num_scalar_prefetch=1,
        grid=(s // t,),
        in_specs=[pl.BlockSpec((None, 8, t), lambda i, ps: (i, 0, 0), memory_space=pltpu.SMEM),
                  pl.BlockSpec((t, D_MODEL), lambda i, ps: (i, 0)), pl.BlockSpec(memory_space=pl.ANY)],
        out_specs=pl.BlockSpec(memory_space=pl.ANY),
        scratch_shapes=[pltpu.SemaphoreType.DMA(())],
    )
    return pl.pallas_call(
        _scatter_kernel,
        grid_spec=grid_spec,
        out_shape=jax.ShapeDtypeStruct(xpad_zeros.shape, F32),
        input_output_aliases={3: 0},
        compiler_params=_cparams("arbitrary"),
        name="moe_scatter",
    )(p_start, ri_t, xn, xpad_zeros)


def _expert_kernel(be_ref, na_ref, x_ref, wgu_ref, wd_ref, y_ref):
    del be_ref
    active = pl.program_id(0) < na_ref[0]

    @pl.when(active)
    def _():
        x = x_ref[...].astype(BF16)
        gu = jnp.dot(x, wgu_ref[...], preferred_element_type=F32)
        g, u = gu[:, :D_EXPERT], gu[:, D_EXPERT:]
        hid = (g * jax.nn.sigmoid(g) * u).astype(BF16)
        y_ref[...] = jnp.dot(hid, wd_ref[...], preferred_element_type=F32)

    @pl.when(jnp.logical_not(active))
    def _():
        y_ref[...] = jnp.zeros_like(y_ref)


def _experts(blk_expert, n_active, xpad, w_gu, w_down):
    nblk = xpad.shape[0] // MOE_BLOCK
    rows = lambda b, be, na: (jnp.minimum(b, na[0] - 1), 0)
    grid_spec = pltpu.PrefetchScalarGridSpec(
        num_scalar_prefetch=2,
        grid=(nblk,),
        in_specs=[pl.BlockSpec((MOE_BLOCK, D_MODEL), rows),
                  pl.BlockSpec((None, D_MODEL, 2 * D_EXPERT), lambda b, be, na: (be[b], 0, 0)),
                  pl.BlockSpec((None, D_EXPERT, D_MODEL), lambda b, be, na: (be[b], 0, 0))],
        out_specs=pl.BlockSpec((MOE_BLOCK, D_MODEL), lambda b, be, na: (b, 0)),
    )
    return pl.pallas_call(
        _expert_kernel,
        grid_spec=grid_spec,
        out_shape=jax.ShapeDtypeStruct(xpad.shape, F32),
        compiler_params=_cparams("arbitrary"),
        name="moe_experts",
    )(blk_expert, n_active, xpad, w_gu, w_down)


def _combine_kernel(ps_ref, ri_ref, rin_ref, h_ref, rf_ref, ypad_ref, out_ref, ybuf, sem):
    t = TOK_TILE
    step = pl.program_id(0)
    cur = step % 2

    def fetch(idx_ref, slot):
        def issue(i, c):
            for k in range(2):
                _row_copy(ypad_ref, _slot(ps_ref, idx_ref, k, i), ybuf.at[slot, k], i,
                          sem.at[slot]).start(priority=k)
            return c

        lax.fori_loop(0, t, issue, 0, unroll=DMA_UNROLL)

    @pl.when(step == 0)
    def _():
        fetch(ri_ref, 0)

    @pl.when(step + 1 < pl.num_programs(0))
    def _():
        fetch(rin_ref, 1 - cur)

    def drain(i, c):
        for k in range(2):
            _row_copy(ypad_ref, 0, ybuf.at[cur, k], 0, sem.at[cur]).wait()
        return c

    lax.fori_loop(0, t, drain, 0, unroll=DMA_UNROLL)
    rf = rf_ref[...]
    out_ref[...] = h_ref[...] + rf[:, 0:1] * ybuf[cur, 0] + rf[:, 1:2] * ybuf[cur, 1]


def _combine(p_start, ri_t, h, rf, ypad):
    s = h.shape[0]
    t = TOK_TILE
    nt = s // t
    idx = lambda shift: pl.BlockSpec((None, 8, t), lambda i, ps: (jnp.minimum(i + shift, nt - 1), 0, 0),
                                     memory_space=pltpu.SMEM)
    grid_spec = pltpu.PrefetchScalarGridSpec(
        num_scalar_prefetch=1,
        grid=(nt,),
        in_specs=[idx(0), idx(1), pl.BlockSpec((t, D_MODEL), lambda i, ps: (i, 0)),
                  pl.BlockSpec((t, LANES), lambda i, ps: (i, 0)), pl.BlockSpec(memory_space=pl.ANY)],
        out_specs=pl.BlockSpec((t, D_MODEL), lambda i, ps: (i, 0)),
        scratch_shapes=[pltpu.VMEM((2, 2, t, D_MODEL), F32), pltpu.SemaphoreType.DMA((2,))],
    )
    return pl.pallas_call(
        _combine_kernel,
        grid_spec=grid_spec,
        out_shape=jax.ShapeDtypeStruct((s, D_MODEL), F32),
        compiler_params=_cparams("arbitrary"),
        name="moe_combine",
    )(p_start, ri_t, ri_t, h, rf, ypad)


def _layer(h3, positions, norm_mix_g, w_in, conv_w, conv_b, lru_wa, lru_ba, lru_wx, lru_bx, lru_lambda,
           q_a_g, w_uq, kv_a_g, w_ukv, q_norm_g, k_norm_g, w_out, norm_ffn_g,
           router_group_w, router_group_b, router_expert_w, router_expert_b, w_gate, w_up, w_down):
    b, s, _ = h3.shape
    assert b == 1 and s % TOK_TILE == 0
    x2 = h3.reshape(s, D_MODEL)
    nt = s // TOK_TILE

    o = np.cumsum((0, D_RNN, D_RNN, Q_LORA, KV_LORA, QK_ROPE))
    w_cat = jnp.concatenate(
        [w_in[:, o[0]:o[4]], w_in[:, o[4]:o[5]], jnp.zeros((D_MODEL, LANES - QK_ROPE), F32), w_in[:, o[5]:]],
        axis=1).astype(BF16)
    row = lambda v: v.reshape(1, -1)
    col = lambda v: v.reshape(-1, 1)
    wq_t = w_uq.T.astype(BF16)
    ukv = w_ukv.reshape(KV_LORA, HEADS, QK_NOPE + V_HEAD)
    wk_t = ukv[:, :, :QK_NOPE].reshape(KV_LORA, HEADS * QK_NOPE).T.astype(BF16)
    wv_t = ukv[:, :, QK_NOPE:].reshape(KV_LORA, HEADS * V_HEAD).T.astype(BF16)
    half = QK_ROPE // 2
    freq = col(ROPE_THETA ** (-jnp.arange(half, dtype=F32) / half))
    wr = jnp.concatenate([router_expert_w, router_group_w,
                          jnp.zeros((D_MODEL, LANES - N_EXPERTS - N_GROUPS), F32)], axis=1)
    wr_hi = wr.astype(BF16)
    wr_lo = (wr - wr_hi.astype(F32)).astype(BF16)
    br = row(jnp.concatenate([router_expert_b, router_group_b, jnp.zeros((LANES - N_EXPERTS - N_GROUPS,), F32)]))
    w_gu = jnp.concatenate([w_gate, w_up], axis=2).astype(BF16)

    x_rnn, g_rnn, q_c, kv_c, pe, gl = _inproj(x2, row(norm_mix_g), w_cat)
    ya_g = _lru(x_rnn, g_rnn, gl, conv_w, row(conv_b), lru_wa.astype(BF16), row(lru_ba),
                lru_wx.astype(BF16), row(lru_bx), row(lru_lambda))
    q_t, k_r, v_t, q_norm, k_norm = _mla_prep(q_c, kv_c, pe, positions.reshape(nt, 1, TOK_TILE), freq, col(q_a_g),
                                              col(kv_a_g), col(q_norm_g), col(k_norm_g), wq_t, wk_t, wv_t)
    bound = jnp.max(jnp.max(q_norm, axis=(1, 2, 3)) * jnp.max(k_norm, axis=(1, 2, 3)))
    yb_t = lax.cond(bound <= ATT_FIXED_REF_LIMIT,
                    lambda: _attention(q_t, k_r, v_t, q_norm, k_norm, True),
                    lambda: _attention(q_t, k_r, v_t, q_norm, k_norm, False))
    h, xn, ri, rf, counts = _outproj_router(x2, ya_g, gl, yb_t, w_out.astype(BF16), row(norm_ffn_g),
                                            wr_hi, wr_lo, br)

    cnt = counts[0, :N_EXPERTS].astype(jnp.int32)
    padded = (cnt + MOE_BLOCK - 1) // MOE_BLOCK * MOE_BLOCK
    p_end = jnp.cumsum(padded)
    p_start = p_end - padded
    n_slots = 2 * s + N_EXPERTS * MOE_BLOCK
    nblk = n_slots // MOE_BLOCK
    blk_first = jnp.arange(nblk, dtype=jnp.int32) * MOE_BLOCK
    blk_expert = jnp.minimum(jnp.sum((p_end[None, :] <= blk_first[:, None]).astype(jnp.int32), axis=1),
                             N_EXPERTS - 1)
    n_active = (p_end[-1:] // MOE_BLOCK).astype(jnp.int32)
    p_start = p_start.astype(jnp.int32)

    xpad = _scatter_rows(p_start, ri, xn, jnp.zeros((n_slots, D_MODEL), F32))
    ypad = _experts(blk_expert, n_active, xpad, w_gu, w_down.astype(BF16))
    out = _combine(p_start, ri, h, rf, ypad)
    return out.reshape(b, s, D_MODEL)


def kernel(x, positions, norm_mix_g, w_in, conv_w, conv_b, lru_wa, lru_ba, lru_wx, lru_bx, lru_lambda, q_a_g,
           w_uq, kv_a_g, w_ukv, q_norm_g, k_norm_g, w_out, norm_ffn_g, router_group_w, router_group_b,
           router_expert_w, router_expert_b, w_gate, w_up, w_down):
    h = x
    for l in range(norm_mix_g.shape[0]):
        h = _layer(h, positions, norm_mix_g[l], w_in[l], conv_w[l], conv_b[l], lru_wa[l], lru_ba[l], lru_wx[l],
                   lru_bx[l], lru_lambda[l], q_a_g[l], w_uq[l], kv_a_g[l], w_ukv[l], q_norm_g[l], k_norm_g[l],
                   w_out[l], norm_ffn_g[l], router_group_w[l], router_group_b[l], router_expert_w[l],
                   router_expert_b[l], w_gate[l], w_up[l], w_down[l])
    return h
```

```python
import functools

import jax
import jax.numpy as jnp
import numpy as np
from jax import lax
from jax.experimental import pallas as pl
from jax.experimental.pallas import tpu as pltpu

F32 = jnp.float32
BF16 = jnp.bfloat16

D_MODEL = 1024
EPS = 1e-6
CHUNK = 64
D_RNN = 1024
LRU_BLOCKS = 4
LRU_BLOCK_W = 256
CONV_W = 4
LRU_C = 8.0
HEADS = 16
V_HEAD = 64
QK_NOPE = 64
QK_ROPE = 32
QK_HEAD = 96
Q_LORA = 768
KV_LORA = 256
ROPE_THETA = 10000.0
N_GROUPS = 4
EXPERTS_PER_GROUP = 8
N_EXPERTS = 32
D_EXPERT = 256
CHUNK_SHIFT = CHUNK.bit_length() - 1
GROUP_SHIFT = EXPERTS_PER_GROUP.bit_length() - 1

LANES = 128
TOK_TILE = 256
ATT_TQ = 512
ATT_HEADS = 4
ATT_FIXED_REF_LIMIT = 48.0
QK_PAD = 128
V_AUG = 80
MOE_BLOCK = 256
DMA_UNROLL = 8
NEG_BIG = -1e30
LOG2E = 1.4426950408889634
VMEM_LIMIT = 56 * 1024 * 1024

_SEG_XR, _SEG_GR, _SEG_QC, _SEG_KV, _SEG_PE, _SEG_GL = 0, 1024, 2048, 2816, 3072, 3200
_N_IN = 5248


def _cparams(*sem):
    return pltpu.CompilerParams(dimension_semantics=sem, vmem_limit_bytes=VMEM_LIMIT)


def _const_spec(shape):
    nd = len(shape)
    return pl.BlockSpec(shape, lambda *_: (0,) * nd)


def _inproj_kernel(x_ref, g_ref, w_ref, xr_ref, gr_ref, qc_ref, kv_ref, pe_ref, gl_ref):
    x = x_ref[...]
    ms = jnp.mean(x * x, axis=-1, keepdims=True)
    u = (x * lax.rsqrt(ms + EPS) * g_ref[...]).astype(BF16)
    p = jnp.dot(u, w_ref[...], preferred_element_type=F32)
    xr_ref[...] = p[:, _SEG_XR:_SEG_GR].astype(BF16)
    gr_ref[...] = p[:, _SEG_GR:_SEG_QC].astype(BF16)
    qc_ref[...] = p[:, _SEG_QC:_SEG_KV].astype(BF16)
    kv_ref[...] = p[:, _SEG_KV:_SEG_PE].astype(BF16)
    pe_ref[...] = p[:, _SEG_PE:_SEG_GL]
    gl_ref[...] = p[:, _SEG_GL:_N_IN].astype(BF16)


def _sigmoid(x):
    return 0.5 * jnp.tanh(0.5 * x) + 0.5


def _lru_kernel(xr_ref, gr_ref, ga_ref, cw_ref, cb_ref, sh_ref, wa_ref, ba_ref, wx_ref, bx_ref, lam_ref,
                out_ref, xprev, hprev):
    t = TOK_TILE

    @pl.when(pl.program_id(0) == 0)
    def _():
        xprev[...] = jnp.zeros((8, D_RNN), F32)
        hprev[...] = jnp.zeros((1, D_RNN), F32)

    x_bf = xr_ref[...]
    x = x_bf.astype(F32)
    xa = cb_ref[...] + cw_ref[CONV_W - 1:CONV_W, :] * x
    top = jnp.zeros((8, D_RNN), F32)
    row8 = lax.broadcasted_iota(jnp.int32, (8, D_RNN), 0)
    for j in range(CONV_W - 1):
        shift = CONV_W - 1 - j
        xa = xa + cw_ref[j:j + 1, :] * jnp.dot(sh_ref[j], x_bf, preferred_element_type=F32)
        top = top + cw_ref[j:j + 1, :] * jnp.where(row8 < shift, pltpu.roll(xprev[...], shift, axis=0), 0.0)
    xa = jnp.concatenate([xa[:8, :] + top, xa[8:, :]], axis=0)
    xprev[...] = x[t - 8:t, :]

    r_parts, i_parts = [], []
    for b in range(LRU_BLOCKS):
        xb = xa[:, b * LRU_BLOCK_W:(b + 1) * LRU_BLOCK_W].astype(BF16)
        r_parts.append(jnp.dot(xb, wa_ref[b], preferred_element_type=F32))
        i_parts.append(jnp.dot(xb, wx_ref[b], preferred_element_type=F32))
    r = _sigmoid(jnp.concatenate(r_parts, axis=1) + ba_ref[...])
    gi = _sigmoid(jnp.concatenate(i_parts, axis=1) + bx_ref[...])
    z = -lam_ref[...]
    softplus = jnp.maximum(z, 0.0) + jnp.log(1.0 + jnp.exp(-jnp.abs(z)))
    log_a = (-LRU_C) * r * softplus
    a = jnp.exp(log_a)
    b_in = jnp.sqrt(1.0 - a * a) * (gi * xa)

    row = lax.broadcasted_iota(jnp.int32, (t, D_RNN), 0)
    d = 1
    while d < 8:
        valid = row >= d
        a_sh = pltpu.roll(a, d, axis=0)
        b_sh = pltpu.roll(b_in, d, axis=0)
        b_in = jnp.where(valid, a * b_sh, 0.0) + b_in
        a = jnp.where(valid, a * a_sh, a)
        d *= 2
    while d < t:
        b_in = jnp.concatenate([b_in[:d], a[d:] * b_in[:t - d] + b_in[d:]], axis=0)
        a = jnp.concatenate([a[:d], a[d:] * a[:t - d]], axis=0)
        d *= 2
    h = a * hprev[...] + b_in
    hprev[...] = h[t - 1:t, :]

    ya = h * jax.nn.gelu(gr_ref[...].astype(F32))
    out_ref[...] = (_sigmoid(ga_ref[...].astype(F32)) * ya).astype(BF16)


def _rope_rows(x, cos, sin):
    half = QK_ROPE // 2
    x1, x2 = x[:, :half, :], x[:, half:, :]
    return x1 * cos - x2 * sin, x1 * sin + x2 * cos


def _mla_prep_kernel(qc_ref, kv_ref, pe_ref, pos_ref, freq_ref, gq_ref, gkv_ref, qg_ref, kg_ref,
                     wq_ref, wk_ref, wv_ref, q_out, k_out, v_out, qn_out, kn_out):
    t = TOK_TILE
    ang = pos_ref[...].astype(F32) * freq_ref[...]
    cos, sin = jnp.cos(ang), jnp.sin(ang)

    qct = qc_ref[...].astype(F32).T
    qcn = qct * lax.rsqrt(jnp.mean(qct * qct, axis=0, keepdims=True) + EPS) * gq_ref[...]
    q = jnp.dot(wq_ref[...], qcn.astype(BF16), preferred_element_type=F32)
    q = q.reshape(HEADS, QK_HEAD, t)
    q = q * lax.rsqrt(jnp.mean(q * q, axis=1, keepdims=True) + EPS) * qg_ref[...][None]
    r1, r2 = _rope_rows(q[:, QK_NOPE:, :], cos, sin)
    scale = (QK_HEAD ** -0.5) * LOG2E
    q_bf = (jnp.concatenate([q[:, :QK_NOPE, :], r1, r2], axis=1) * scale).astype(BF16)
    q_out[...] = jnp.concatenate([q_bf, jnp.zeros((HEADS, QK_PAD - QK_HEAD, t), BF16)], axis=1)
    q_f = q_bf.astype(F32)
    qn_out[...] = jnp.sqrt(jnp.sum(q_f * q_f, axis=1, keepdims=True))

    kvt = kv_ref[...].astype(F32).T
    kvn = (kvt * lax.rsqrt(jnp.mean(kvt * kvt, axis=0, keepdims=True) + EPS) * gkv_ref[...]).astype(BF16)
    kn = jnp.dot(wk_ref[...], kvn, preferred_element_type=F32).reshape(HEADS, QK_NOPE, t)
    v = jnp.dot(wv_ref[...], kvn, preferred_element_type=F32).reshape(HEADS, V_HEAD, t)
    pe = pe_ref[...].T[:QK_ROPE, :]
    ssq = jnp.sum(kn * kn, axis=1, keepdims=True) + jnp.sum(pe * pe, axis=0, keepdims=True)[None]
    rstd = lax.rsqrt(ssq * (1.0 / QK_HEAD) + EPS)
    kg = kg_ref[...]
    k_nope = kn * rstd * kg[None, :QK_NOPE, :]
    k_pe = (pe * kg[QK_NOPE:, :])[None] * rstd
    p1, p2 = _rope_rows(k_pe, cos, sin)
    k_real = jnp.concatenate([k_nope, p1, p2], axis=1)
    pad = jnp.where(lax.broadcasted_iota(jnp.int32, (HEADS, QK_PAD - QK_HEAD, t), 1) == 0, 1.0, 0.0)
    kt = jnp.concatenate([k_real, pad], axis=1)
    for h in range(HEADS):
        k_out[h] = kt[h].T.astype(BF16)
    k_f = k_real.astype(BF16).astype(F32)
    k_max = jnp.max(jnp.sum(k_f * k_f, axis=1, keepdims=True), axis=2, keepdims=True)
    kn_out[...] = jnp.broadcast_to(jnp.sqrt(k_max), (HEADS, 1, LANES))
    v_out[...] = jnp.concatenate([v, jnp.ones((HEADS, V_AUG - V_HEAD, t), F32)], axis=1).astype(BF16)


def _token_mix_prep_kernel(x_ref, g_ref, w_ref, cw_ref, cb_ref, sh_ref, wa_ref, ba_ref, wx_ref, bx_ref, lam_ref,
                           pos_ref, freq_ref, gq_ref, gkv_ref, qg_ref, kg_ref, wq_ref, wk_ref, wv_ref,
                           ya_ref, gl_ref, q_out, k_out, v_out, qn_out, kn_out,
                           xr_s, gr_s, qc_s, kv_s, pe_s, xprev, hprev):
    _inproj_kernel(x_ref, g_ref, w_ref, xr_s, gr_s, qc_s, kv_s, pe_s, gl_ref)
    _lru_kernel(xr_s, gr_s, gl_ref.at[:, pl.ds(0, D_MODEL)], cw_ref, cb_ref, sh_ref, wa_ref, ba_ref, wx_ref, bx_ref,
                lam_ref, ya_ref, xprev, hprev)
    _mla_prep_kernel(qc_s, kv_s, pe_s, pos_ref, freq_ref, gq_ref, gkv_ref, qg_ref, kg_ref, wq_ref, wk_ref, wv_ref,
                     q_out, k_out, v_out, qn_out, kn_out)


def _token_mix_prep(x2, g, w_cat, conv_w, conv_b, wa, ba, wx, bx, lam, pos_row, freq, gq, gkv, qg, kg,
                    wq_t, wk_t, wv_t):
    s = x2.shape[0]
    t = TOK_TILE
    nt = s // t
    row = lambda n: pl.BlockSpec((t, n), lambda i: (i, 0))
    blk = lambda r, c: pl.BlockSpec((HEADS, None, r, c), lambda i: (0, i, 0, 0))
    lru_w = _const_spec((LRU_BLOCKS, LRU_BLOCK_W, LRU_BLOCK_W))
    shifts = jnp.stack([jnp.eye(t, k=-(CONV_W - 1 - j), dtype=BF16) for j in range(CONV_W - 1)])
    return pl.pallas_call(
        _token_mix_prep_kernel,
        grid=(nt,),
        in_specs=[row(D_MODEL), _const_spec((1, D_MODEL)),
                  pl.BlockSpec((D_MODEL, _N_IN), lambda i: (0, 0), pipeline_mode=pl.Buffered(1)),
                  _const_spec((CONV_W, D_RNN)), _const_spec((1, D_RNN)), _const_spec((CONV_W - 1, t, t)),
                  lru_w, _const_spec((1, D_RNN)), lru_w, _const_spec((1, D_RNN)), _const_spec((1, D_RNN)),
                  pl.BlockSpec((None, 1, t), lambda i: (i, 0, 0)),
                  _const_spec((QK_ROPE // 2, 1)), _const_spec((Q_LORA, 1)), _const_spec((KV_LORA, 1)),
                  _const_spec((QK_HEAD, 1)), _const_spec((QK_HEAD, 1)),
                  _const_spec((HEADS * QK_HEAD, Q_LORA)), _const_spec((HEADS * QK_NOPE, KV_LORA)),
                  _const_spec((HEADS * V_HEAD, KV_LORA))],
        out_specs=[row(D_RNN), row(2 * D_MODEL),
                   blk(QK_PAD, t), blk(t, QK_PAD), blk(V_AUG, t), blk(1, t), blk(1, LANES)],
        out_shape=[jax.ShapeDtypeStruct((s, D_RNN), BF16), jax.ShapeDtypeStruct((s, 2 * D_MODEL), BF16),
                   jax.ShapeDtypeStruct((HEADS, nt, QK_PAD, t), BF16),
                   jax.ShapeDtypeStruct((HEADS, nt, t, QK_PAD), BF16),
                   jax.ShapeDtypeStruct((HEADS, nt, V_AUG, t), BF16),
                   jax.ShapeDtypeStruct((HEADS, nt, 1, t), F32),
                   jax.ShapeDtypeStruct((HEADS, nt, 1, LANES), F32)],
        scratch_shapes=[pltpu.VMEM((t, D_RNN), BF16), pltpu.VMEM((t, D_RNN), BF16), pltpu.VMEM((t, Q_LORA), BF16),
                        pltpu.VMEM((t, KV_LORA), BF16), pltpu.VMEM((t, LANES), F32),
                        pltpu.VMEM((8, D_RNN), F32), pltpu.VMEM((1, D_RNN), F32)],
        compiler_params=_cparams("arbitrary"),
        name="token_mix_prep",
    )(x2, g, w_cat, conv_w, conv_b, shifts, wa, ba, wx, bx, lam, pos_row, freq, gq, gkv, qg, kg, wq_t, wk_t, wv_t)


def _attn_kernel(fixed_ref, *refs):
    if fixed_ref:
        q_ref, k_ref, v_ref, qn_ref, kn_ref, o_ref, s_scr, p_scr = refs
    else:
        q_ref, k_ref, v_ref, o_ref, s_scr = refs
    tq, tk = ATT_TQ, TOK_TILE
    nsub = tq // tk
    qi = pl.program_id(1)

    def query_block(g, a):
        q = q_ref[g, a]
        if not fixed_ref:
            return q
        k_max = jnp.max(kn_ref[g], axis=0)[:, :1]
        ref = qn_ref[g, a] * k_max
        row = lax.broadcasted_iota(jnp.int32, (QK_PAD - QK_HEAD, tk), 0)
        extra = jnp.where(row == 0, -ref, 0.0).astype(BF16)
        return jnp.concatenate([q[:QK_HEAD, :], extra], axis=0)

    qs = [jnp.concatenate([query_block(g, a) for a in range(nsub)], axis=1) for g in range(ATT_HEADS)]

    def finish(accs):
        for g in range(ATT_HEADS):
            o = accs[g][:V_HEAD, :] / accs[g][V_HEAD:V_HEAD + 1, :]
            for a in range(nsub):
                o_ref[g, a] = o[:, a * tk:(a + 1) * tk]

    if fixed_ref:
        def fixed_step(kb, slot, accs, tail=None, last=False):
            if not last:
                for g in range(ATT_HEADS):
                    s_scr[1 - slot, g] = jnp.dot(k_ref[g, kb + 1], qs[g], preferred_element_type=F32)
            prev = jnp.maximum(kb - 1, 0)
            accs = tuple(accs[g] + jnp.dot(v_ref[g, prev], p_scr[1 - slot, g], preferred_element_type=F32)
                         for g in range(ATT_HEADS))
            for g in range(ATT_HEADS):
                s = s_scr[slot, g]
                if tail is not None:
                    kk = (lax.broadcasted_iota(jnp.int32, (tk, tq), 0) + tail * tk) >> CHUNK_SHIFT
                    qq = lax.broadcasted_iota(jnp.int32, (tk, tq), 1) >> CHUNK_SHIFT
                    s = jnp.where(kk <= qq, s, NEG_BIG)
                p_scr[slot, g] = jnp.exp2(s).astype(BF16)
            return accs

        def fixed_pair(i, accs):
            return fixed_step(2 * i + 1, 1, fixed_step(2 * i, 0, accs))

        assert nsub % 2 == 0
        p_scr[1] = jnp.zeros(p_scr.shape[1:], BF16)
        for g in range(ATT_HEADS):
            s_scr[0, g] = jnp.dot(k_ref[g, 0], qs[g], preferred_element_type=F32)
        accs = tuple(jnp.zeros((V_AUG, tq), F32) for _ in range(ATT_HEADS))
        accs = lax.fori_loop(0, qi * (nsub // 2), fixed_pair, accs)
        kb = qi * nsub
        for t in range(nsub):
            accs = fixed_step(kb + t, t % 2, accs, tail=t, last=(t == nsub - 1))
        last_slot = (nsub - 1) % 2
        finish(tuple(accs[g] + jnp.dot(v_ref[g, kb + nsub - 1], p_scr[last_slot, g], preferred_element_type=F32)
                     for g in range(ATT_HEADS)))
        return

    def prefetch(kb, slot):
        for g in range(ATT_HEADS):
            s_scr[slot, g] = jnp.dot(k_ref[g, kb], qs[g], preferred_element_type=F32)

    def consume(kb, slot, state, tail):
        out = []
        for g in range(ATT_HEADS):
            m, acc = state[g]
            if tail is None:
                visible = None
            else:
                kk = (lax.broadcasted_iota(jnp.int32, (tk, tq), 0) + tail * tk) >> CHUNK_SHIFT
                qq = lax.broadcasted_iota(jnp.int32, (tk, tq), 1) >> CHUNK_SHIFT
                visible = kk <= qq
            load = lambda: s_scr[slot, g] if visible is None else jnp.where(visible, s_scr[slot, g], NEG_BIG)
            m_new = jnp.maximum(m, jnp.max(load(), axis=0, keepdims=True))
            p = jnp.exp2(load() - m_new).astype(BF16)
            acc = acc * jnp.exp2(m - m_new) + jnp.dot(v_ref[g, kb], p, preferred_element_type=F32)
            out.append((m_new, acc))
        return tuple(out)

    def pair(i, state):
        for t in range(2):
            prefetch(2 * i + t + 1, (t + 1) % 2)
            state = consume(2 * i + t, t, state, None)
        return state

    assert nsub % 2 == 0
    state = tuple((jnp.full((1, tq), NEG_BIG, F32), jnp.zeros((V_AUG, tq), F32)) for _ in range(ATT_HEADS))
    prefetch(0, 0)
    state = lax.fori_loop(0, qi * (nsub // 2), pair, state)
    kb = qi * nsub
    for t in range(nsub):
        if t + 1 < nsub:
            prefetch(kb + t + 1, (t + 1) % 2)
        state = consume(kb + t, t % 2, state, t)
    finish(tuple(acc for _, acc in state))


def _attention(q_t, k, v_t, q_norm, k_norm, fixed_ref):
    heads, nt = q_t.shape[0], q_t.shape[1]
    s = nt * TOK_TILE
    assert s % ATT_TQ == 0 and heads % ATT_HEADS == 0
    qsub = ATT_TQ // TOK_TILE
    resident = dict(pipeline_mode=pl.Buffered(1))
    in_specs = [pl.BlockSpec((ATT_HEADS, qsub, QK_PAD, TOK_TILE), lambda h, i: (h, i, 0, 0)),
                pl.BlockSpec((ATT_HEADS, nt, TOK_TILE, QK_PAD), lambda h, i: (h, 0, 0, 0), **resident),
                pl.BlockSpec((ATT_HEADS, nt, V_AUG, TOK_TILE), lambda h, i: (h, 0, 0, 0), **resident)]
    args = (q_t, k, v_t)
    scratch = [pltpu.VMEM((2, ATT_HEADS, TOK_TILE, ATT_TQ), F32)]
    if fixed_ref:
        in_specs += [pl.BlockSpec((ATT_HEADS, qsub, 1, TOK_TILE), lambda h, i: (h, i, 0, 0)),
                     pl.BlockSpec((ATT_HEADS, nt, 1, LANES), lambda h, i: (h, 0, 0, 0))]
        args += (q_norm, k_norm)
        scratch += [pltpu.VMEM((2, ATT_HEADS, TOK_TILE, ATT_TQ), BF16)]
    return pl.pallas_call(
        functools.partial(_attn_kernel, fixed_ref),
        grid=(heads // ATT_HEADS, s // ATT_TQ),
        in_specs=in_specs,
        out_specs=pl.BlockSpec((ATT_HEADS, qsub, V_HEAD, TOK_TILE), lambda h, i: (h, i, 0, 0)),
        out_shape=jax.ShapeDtypeStruct((heads, nt, V_HEAD, TOK_TILE), F32),
        scratch_shapes=scratch,
        compiler_params=_cparams("arbitrary", "arbitrary"),
        name="attention_fixed_ref" if fixed_ref else "attention_online",
    )(*args)


def _first_lane(cond, lane):
    return jnp.min(jnp.where(cond, lane.astype(F32), 4.0 * LANES), axis=1, keepdims=True).astype(jnp.int32)


def _outproj_router_kernel(x_ref, ya_ref, gb_ref, yb_ref, wo_ref, gf_ref, wrh_ref, wrl_ref, br_ref,
                           h_ref, xn_ref, ri_ref, rf_ref, cnt_ref, base):
    t = TOK_TILE

    @pl.when(pl.program_id(0) == 0)
    def _():
        base[...] = jnp.zeros((1, LANES), F32)

    yb = yb_ref[...].reshape(HEADS * V_HEAD, t).T
    merged = ya_ref[...].astype(F32) + _sigmoid(gb_ref[...].astype(F32)) * yb
    h = x_ref[...] + jnp.dot(merged.astype(BF16), wo_ref[...], preferred_element_type=F32)
    h_ref[...] = h
    xn = h * lax.rsqrt(jnp.mean(h * h, axis=-1, keepdims=True) + EPS) * gf_ref[...]
    xn_ref[...] = xn

    x_hi = xn.astype(BF16)
    x_lo = (xn - x_hi.astype(F32)).astype(BF16)
    logits = (jnp.dot(x_hi, wrh_ref[...], preferred_element_type=F32)
              + jnp.dot(x_hi, wrl_ref[...], preferred_element_type=F32)
              + jnp.dot(x_lo, wrh_ref[...], preferred_element_type=F32)) + br_ref[...]

    lane = lax.broadcasted_iota(jnp.int32, (t, LANES), 1)
    is_grp = jnp.logical_and(lane >= N_EXPERTS, lane < N_EXPERTS + N_GROUPS)
    gl = jnp.where(is_grp, logits, NEG_BIG)
    gmax = jnp.max(gl, axis=1, keepdims=True)
    grp = _first_lane(gl == gmax, lane) - N_EXPERTS
    p_grp = 1.0 / jnp.sum(jnp.where(is_grp, jnp.exp(gl - gmax), 0.0), axis=1, keepdims=True)
    el = jnp.where((lane >> GROUP_SHIFT) == grp, logits, NEG_BIG)
    v1 = jnp.max(el, axis=1, keepdims=True)
    e1 = _first_lane(el == v1, lane)
    el2 = jnp.where(lane == e1, NEG_BIG, el)
    v2 = jnp.max(el2, axis=1, keepdims=True)
    e2 = _first_lane(el2 == v2, lane)
    ex = jnp.exp(v2 - v1)
    w1 = p_grp / (1.0 + ex)
    w2 = p_grp * ex / (1.0 + ex)

    tri = (lax.broadcasted_iota(jnp.int32, (t, t), 1) < lax.broadcasted_iota(jnp.int32, (t, t), 0))
    tri = jnp.where(tri, 1.0, 0.0).astype(BF16)
    oh1 = jnp.where(lane == e1, 1.0, 0.0)
    oh2 = jnp.where(lane == e2, 1.0, 0.0)
    pre1 = jnp.dot(tri, oh1.astype(BF16), preferred_element_type=F32) + base[...]
    base1 = base[...] + jnp.sum(oh1, axis=0, keepdims=True)
    pre2 = jnp.dot(tri, oh2.astype(BF16), preferred_element_type=F32) + base1
    base2 = base1 + jnp.sum(oh2, axis=0, keepdims=True)
    r1 = jnp.sum(oh1 * pre1, axis=1, keepdims=True).astype(jnp.int32)
    r2 = jnp.sum(oh2 * pre2, axis=1, keepdims=True).astype(jnp.int32)
    base[...] = base2
    cnt_ref[...] = base2

    ri = jnp.where(lane == 0, e1, jnp.where(lane == 1, e2, jnp.where(lane == 2, r1, r2)))
    ri_ref[...] = ri.astype(F32).T[:8, :].astype(jnp.int32)
    rf_ref[...] = jnp.where(lane == 0, w1, w2)


def _outproj_router(x2, ya_g, gl, yb_t, w_out, gf, wr_hi, wr_lo, br):
    s = x2.shape[0]
    t = TOK_TILE
    row = lambda n: pl.BlockSpec((t, n), lambda i: (i, 0))
    return pl.pallas_call(
        _outproj_router_kernel,
        grid=(s // t,),
        in_specs=[row(D_MODEL), row(D_MODEL), pl.BlockSpec((t, D_MODEL), lambda i: (i, 1)),
                  pl.BlockSpec((HEADS, None, V_HEAD, t), lambda i: (0, i, 0, 0)),
                  _const_spec((D_MODEL, D_MODEL)), _const_spec((1, D_MODEL)),
                  _const_spec((D_MODEL, LANES)), _const_spec((D_MODEL, LANES)), _const_spec((1, LANES))],
        out_specs=[row(D_MODEL), row(D_MODEL), pl.BlockSpec((None, 8, t), lambda i: (i, 0, 0)), row(LANES),
                   _const_spec((1, LANES))],
        out_shape=[jax.ShapeDtypeStruct((s, D_MODEL), F32), jax.ShapeDtypeStruct((s, D_MODEL), F32),
                   jax.ShapeDtypeStruct((s // t, 8, t), jnp.int32), jax.ShapeDtypeStruct((s, LANES), F32),
                   jax.ShapeDtypeStruct((1, LANES), F32)],
        scratch_shapes=[pltpu.VMEM((1, LANES), F32)],
        compiler_params=_cparams("arbitrary"),
        name="outproj_router",
    )(x2, ya_g, gl, yb_t, w_out, gf, wr_hi, wr_lo, br)


def _row_copy(src_ref, src_row, dst_ref, dst_row, sem):
    return pltpu.make_async_copy(src_ref.at[pl.ds(src_row, 1), :], dst_ref.at[pl.ds(dst_row, 1), :], sem)


def _slot(ps_ref, ri_ref, k, i):
    return ps_ref[ri_ref[k, i]] + ri_ref[2 + k, i]


def _scatter_kernel(ps_ref, ri_ref, xn_ref, zeros_ref, xpad_ref, sem):
    del zeros_ref
    t = TOK_TILE

    def issue(i, c):
        for k in range(2):
            _row_copy(xn_ref, i, xpad_ref, _slot(ps_ref, ri_ref, k, i), sem).start(priority=k)
        return c

    lax.fori_loop(0, t, issue, 0, unroll=DMA_UNROLL)

    def drain(i, c):
        for _ in range(2):
            _row_copy(xn_ref, 0, xpad_ref, 0, sem).wait()
        return c

    lax.fori_loop(0, t, drain, 0, unroll=DMA_UNROLL)


def _scatter_rows(p_start, ri_t, xn, xpad_zeros):
    s = xn.shape[0]
    t = TOK_TILE
    grid_spec = pltpu.PrefetchScalarGridSpec(
        num_scalar_prefetch=1,
        grid=(s // t,),
        in_specs=[pl.BlockSpec((None, 8, t), lambda i, ps: (i, 0, 0), memory_space=pltpu.SMEM),
                  pl.BlockSpec((t, D_MODEL), lambda i, ps: (i, 0)), pl.BlockSpec(memory_space=pl.ANY)],
        out_specs=pl.BlockSpec(memory_space=pl.ANY),
        scratch_shapes=[pltpu.SemaphoreType.DMA(())],
    )
    return pl.pallas_call(
        _scatter_kernel,
        grid_spec=grid_spec,
        out_shape=jax.ShapeDtypeStruct(xpad_zeros.shape, F32),
        input_output_aliases={3: 0},
        compiler_params=_cparams("arbitrary"),
        name="moe_scatter",
    )(p_start, ri_t, xn, xpad_zeros)


def _expert_kernel(be_ref, na_ref, x_ref, wg_ref, wu_ref, wd_ref, y_ref, wgu_bf, wd_bf):
    b = pl.program_id(0)
    active = b < na_ref[0]

    @pl.when(jnp.logical_and(active, jnp.logical_or(b == 0, be_ref[b] != be_ref[jnp.maximum(b - 1, 0)])))
    def _():
        wgu_bf[:, :D_EXPERT] = wg_ref[...].astype(BF16)
        wgu_bf[:, D_EXPERT:] = wu_ref[...].astype(BF16)
        wd_bf[...] = wd_ref[...].astype(BF16)

    @pl.when(active)
    def _():
        x = x_ref[...].astype(BF16)
        gu = jnp.dot(x, wgu_bf[...], preferred_element_type=F32)
        g, u = gu[:, :D_EXPERT], gu[:, D_EXPERT:]
        hid = (g * _sigmoid(g) * u).astype(BF16)
        y_ref[...] = jnp.dot(hid, wd_bf[...], preferred_element_type=F32)

    @pl.when(jnp.logical_not(active))
    def _():
        y_ref[...] = jnp.zeros_like(y_ref)


def _experts(blk_expert, n_active, xpad, w_gate, w_up, w_down):
    nblk = xpad.shape[0] // MOE_BLOCK
    rows = lambda b, be, na: (jnp.minimum(b, na[0] - 1), 0)
    expert = lambda b, be, na: (be[b], 0, 0)
    grid_spec = pltpu.PrefetchScalarGridSpec(
        num_scalar_prefetch=2,
        grid=(nblk,),
        in_specs=[pl.BlockSpec((MOE_BLOCK, D_MODEL), rows),
                  pl.BlockSpec((None, D_MODEL, D_EXPERT), expert), pl.BlockSpec((None, D_MODEL, D_EXPERT), expert),
                  pl.BlockSpec((None, D_EXPERT, D_MODEL), expert)],
        out_specs=pl.BlockSpec((MOE_BLOCK, D_MODEL), lambda b, be, na: (b, 0)),
        scratch_shapes=[pltpu.VMEM((D_MODEL, 2 * D_EXPERT), BF16), pltpu.VMEM((D_EXPERT, D_MODEL), BF16)],
    )
    return pl.pallas_call(
        _expert_kernel,
        grid_spec=grid_spec,
        out_shape=jax.ShapeDtypeStruct(xpad.shape, F32),
        compiler_params=_cparams("arbitrary"),
        name="moe_experts",
    )(blk_expert, n_active, xpad, w_gate, w_up, w_down)


def _combine_kernel(ps_ref, ri_ref, rin_ref, h_ref, rf_ref, ypad_ref, out_ref, ybuf, sem):
    t = TOK_TILE
    step = pl.program_id(0)
    cur = step % 2

    def fetch(idx_ref, slot):
        def issue(i, c):
            for k in range(2):
                _row_copy(ypad_ref, _slot(ps_ref, idx_ref, k, i), ybuf.at[slot, k], i,
                          sem.at[slot]).start(priority=k)
            return c

        lax.fori_loop(0, t, issue, 0, unroll=DMA_UNROLL)

    @pl.when(step == 0)
    def _():
        fetch(ri_ref, 0)

    @pl.when(step + 1 < pl.num_programs(0))
    def _():
        fetch(rin_ref, 1 - cur)

    def drain(i, c):
        for k in range(2):
            _row_copy(ypad_ref, 0, ybuf.at[cur, k], 0, sem.at[cur]).wait()
        return c

    lax.fori_loop(0, t, drain, 0, unroll=DMA_UNROLL)
    rf = rf_ref[...]
    out_ref[...] = h_ref[...] + rf[:, 0:1] * ybuf[cur, 0] + rf[:, 1:2] * ybuf[cur, 1]


def _combine(p_start, ri_t, h, rf, ypad):
    s = h.shape[0]
    t = TOK_TILE
    nt = s // t
    idx = lambda shift: pl.BlockSpec((None, 8, t), lambda i, ps: (jnp.minimum(i + shift, nt - 1), 0, 0),
                                     memory_space=pltpu.SMEM)
    grid_spec = pltpu.PrefetchScalarGridSpec(
        num_scalar_prefetch=1,
        grid=(nt,),
        in_specs=[idx(0), idx(1), pl.BlockSpec((t, D_MODEL), lambda i, ps: (i, 0)),
                  pl.BlockSpec((t, LANES), lambda i, ps: (i, 0)), pl.BlockSpec(memory_space=pl.ANY)],
        out_specs=pl.BlockSpec((t, D_MODEL), lambda i, ps: (i, 0)),
        scratch_shapes=[pltpu.VMEM((2, 2, t, D_MODEL), F32), pltpu.SemaphoreType.DMA((2,))],
    )
    return pl.pallas_call(
        _combine_kernel,
        grid_spec=grid_spec,
        out_shape=jax.ShapeDtypeStruct((s, D_MODEL), F32),
        compiler_params=_cparams("arbitrary"),
        name="moe_combine",
    )(p_start, ri_t, ri_t, h, rf, ypad)


def _layer(h3, positions, norm_mix_g, w_in, conv_w, conv_b, lru_wa, lru_ba, lru_wx, lru_bx, lru_lambda,
           q_a_g, w_uq, kv_a_g, w_ukv, q_norm_g, k_norm_g, w_out, norm_ffn_g,
           router_group_w, router_group_b, router_expert_w, router_expert_b, w_gate, w_up, w_down):
    b, s, _ = h3.shape
    assert b == 1 and s % TOK_TILE == 0
    x2 = h3.reshape(s, D_MODEL)
    nt = s // TOK_TILE

    o = np.cumsum((0, D_RNN, D_RNN, Q_LORA, KV_LORA, QK_ROPE))
    w_cat = jnp.concatenate(
        [w_in[:, o[0]:o[4]], w_in[:, o[4]:o[5]], jnp.zeros((D_MODEL, LANES - QK_ROPE), F32), w_in[:, o[5]:]],
        axis=1).astype(BF16)
    row = lambda v: v.reshape(1, -1)
    col = lambda v: v.reshape(-1, 1)
    wq_t = w_uq.T.astype(BF16)
    ukv = w_ukv.reshape(KV_LORA, HEADS, QK_NOPE + V_HEAD)
    wk_t = ukv[:, :, :QK_NOPE].reshape(KV_LORA, HEADS * QK_NOPE).T.astype(BF16)
    wv_t = ukv[:, :, QK_NOPE:].reshape(KV_LORA, HEADS * V_HEAD).T.astype(BF16)
    half = QK_ROPE // 2
    freq = col(ROPE_THETA ** (-jnp.arange(half, dtype=F32) / half))
    wr = jnp.concatenate([router_expert_w, router_group_w,
                          jnp.zeros((D_MODEL, LANES - N_EXPERTS - N_GROUPS), F32)], axis=1)
    wr_hi = wr.astype(BF16)
    wr_lo = (wr - wr_hi.astype(F32)).astype(BF16)
    br = row(jnp.concatenate([router_expert_b, router_group_b, jnp.zeros((LANES - N_EXPERTS - N_GROUPS,), F32)]))

    ya_g, gl, q_t, k_r, v_t, q_norm, k_norm = _token_mix_prep(
        x2, row(norm_mix_g), w_cat, conv_w, row(conv_b), lru_wa.astype(BF16), row(lru_ba), lru_wx.astype(BF16),
        row(lru_bx), row(lru_lambda), positions.reshape(nt, 1, TOK_TILE), freq, col(q_a_g), col(kv_a_g),
        col(q_norm_g), col(k_norm_g), wq_t, wk_t, wv_t)
    bound = jnp.max(jnp.max(q_norm, axis=(1, 2, 3)) * jnp.max(k_norm, axis=(1, 2, 3)))
    yb_t = lax.cond(bound <= ATT_FIXED_REF_LIMIT,
                    lambda: _attention(q_t, k_r, v_t, q_norm, k_norm, True),
                    lambda: _attention(q_t, k_r, v_t, q_norm, k_norm, False))
    h, xn, ri, rf, counts = _outproj_router(x2, ya_g, gl, yb_t, w_out.astype(BF16), row(norm_ffn_g),
                                            wr_hi, wr_lo, br)

    cnt = counts[0, :N_EXPERTS].astype(jnp.int32)
    padded = (cnt + MOE_BLOCK - 1) // MOE_BLOCK * MOE_BLOCK
    p_end = jnp.cumsum(padded)
    p_start = p_end - padded
    n_slots = 2 * s + N_EXPERTS * MOE_BLOCK
    nblk = n_slots // MOE_BLOCK
    blk_first = jnp.arange(nblk, dtype=jnp.int32) * MOE_BLOCK
    blk_expert = jnp.minimum(jnp.sum((p_end[None, :] <= blk_first[:, None]).astype(jnp.int32), axis=1),
                             N_EXPERTS - 1)
    n_active = (p_end[-1:] // MOE_BLOCK).astype(jnp.int32)
    p_start = p_start.astype(jnp.int32)

    xpad = _scatter_rows(p_start, ri, xn, jnp.zeros((n_slots, D_MODEL), F32))
    ypad = _experts(blk_expert, n_active, xpad, w_gate, w_up, w_down)
    out = _combine(p_start, ri, h, rf, ypad)
    return out.reshape(b, s, D_MODEL)


def kernel(x, positions, norm_mix_g, w_in, conv_w, conv_b, lru_wa, lru_ba, lru_wx, lru_bx, lru_lambda, q_a_g,
           w_uq, kv_a_g, w_ukv, q_norm_g, k_norm_g, w_out, norm_ffn_g, router_group_w, router_group_b,
           router_expert_w, router_expert_b, w_gate, w_up, w_down):
    h = x
    for l in range(norm_mix_g.shape[0]):
        h = _layer(h, positions, norm_mix_g[l], w_in[l], conv_w[l], conv_b[l], lru_wa[l], lru_ba[l], lru_wx[l],
                   lru_bx[l], lru_lambda[l], q_a_g[l], w_uq[l], kv_a_g[l], w_ukv[l], q_norm_g[l], k_norm_g[l],
                   w_out[l], norm_ffn_g[l], router_group_w[l], router_group_b[l], router_expert_w[l],
                   router_expert_b[l], w_gate[l], w_up[l], w_down[l])
    return h
```

```python
import functools

import jax
import jax.numpy as jnp
import numpy as np
from jax import lax
from jax.experimental import pallas as pl
from jax.experimental.pallas import tpu as pltpu

F32 = jnp.float32
BF16 = jnp.bfloat16

D_MODEL = 1024
EPS = 1e-6
CHUNK = 64
D_RNN = 1024
LRU_BLOCKS = 4
LRU_BLOCK_W = 256
CONV_W = 4
LRU_C = 8.0
HEADS = 16
V_HEAD = 64
QK_NOPE = 64
QK_ROPE = 32
QK_HEAD = 96
Q_LORA = 768
KV_LORA = 256
ROPE_THETA = 10000.0
N_GROUPS = 4
EXPERTS_PER_GROUP = 8
N_EXPERTS = 32
D_EXPERT = 256
CHUNK_SHIFT = CHUNK.bit_length() - 1
GROUP_SHIFT = EXPERTS_PER_GROUP.bit_length() - 1

LANES = 128
TOK_TILE = 256
ATT_TQ = 512
ATT_HEADS = 4
ATT_FIXED_REF_LIMIT = 48.0
QK_PAD = 128
V_AUG = 80
MOE_BLOCK = 256
DMA_UNROLL = 8
NEG_BIG = -1e30
LOG2E = 1.4426950408889634
VMEM_LIMIT = 56 * 1024 * 1024

_SEG_XR, _SEG_GR, _SEG_QC, _SEG_KV, _SEG_PE, _SEG_GL = 0, 1024, 2048, 2816, 3072, 3200
_N_IN = 5248


def _cparams(*sem):
    return pltpu.CompilerParams(dimension_semantics=sem, vmem_limit_bytes=VMEM_LIMIT)


def _const_spec(shape):
    nd = len(shape)
    return pl.BlockSpec(shape, lambda *_: (0,) * nd)


def _inproj_kernel(x_ref, g_ref, w_ref, xr_ref, gr_ref, qc_ref, kv_ref, pe_ref, gl_ref):
    x = x_ref[...]
    ms = jnp.mean(x * x, axis=-1, keepdims=True)
    u = (x * lax.rsqrt(ms + EPS) * g_ref[...]).astype(BF16)
    p = jnp.dot(u, w_ref[...], preferred_element_type=F32)
    xr_ref[...] = p[:, _SEG_XR:_SEG_GR].astype(BF16)
    gr_ref[...] = p[:, _SEG_GR:_SEG_QC].astype(BF16)
    qc_ref[...] = p[:, _SEG_QC:_SEG_KV].astype(BF16)
    kv_ref[...] = p[:, _SEG_KV:_SEG_PE].astype(BF16)
    pe_ref[...] = p[:, _SEG_PE:_SEG_GL]
    gl_ref[...] = p[:, _SEG_GL:_N_IN].astype(BF16)


def _sigmoid(x):
    return 1.0 / (1.0 + jnp.exp(-x))


def _lru_kernel(xr_ref, gr_ref, ga_ref, cw_ref, cb_ref, sh_ref, wa_ref, ba_ref, wx_ref, bx_ref, lam_ref,
                out_ref, xprev, hprev):
    t = TOK_TILE

    @pl.when(pl.program_id(0) == 0)
    def _():
        xprev[...] = jnp.zeros((8, D_RNN), F32)
        hprev[...] = jnp.zeros((1, D_RNN), F32)

    x_bf = xr_ref[...]
    x = x_bf.astype(F32)
    xa = cb_ref[...] + cw_ref[CONV_W - 1:CONV_W, :] * x
    top = jnp.zeros((8, D_RNN), F32)
    row8 = lax.broadcasted_iota(jnp.int32, (8, D_RNN), 0)
    for j in range(CONV_W - 1):
        shift = CONV_W - 1 - j
        xa = xa + cw_ref[j:j + 1, :] * jnp.dot(sh_ref[j], x_bf, preferred_element_type=F32)
        top = top + cw_ref[j:j + 1, :] * jnp.where(row8 < shift, pltpu.roll(xprev[...], shift, axis=0), 0.0)
    xa = jnp.concatenate([xa[:8, :] + top, xa[8:, :]], axis=0)
    xprev[...] = x[t - 8:t, :]

    r_parts, i_parts = [], []
    for b in range(LRU_BLOCKS):
        xb = xa[:, b * LRU_BLOCK_W:(b + 1) * LRU_BLOCK_W].astype(BF16)
        r_parts.append(jnp.dot(xb, wa_ref[b], preferred_element_type=F32))
        i_parts.append(jnp.dot(xb, wx_ref[b], preferred_element_type=F32))
    r = _sigmoid(jnp.concatenate(r_parts, axis=1) + ba_ref[...])
    gi = _sigmoid(jnp.concatenate(i_parts, axis=1) + bx_ref[...])
    z = -lam_ref[...]
    softplus = jnp.maximum(z, 0.0) + jnp.log(1.0 + jnp.exp(-jnp.abs(z)))
    log_a = (-LRU_C) * r * softplus
    a = jnp.exp(log_a)
    b_in = jnp.sqrt(1.0 - a * a) * (gi * xa)

    row = lax.broadcasted_iota(jnp.int32, (t, D_RNN), 0)
    d = 1
    while d < 8:
        valid = row >= d
        a_sh = pltpu.roll(a, d, axis=0)
        b_sh = pltpu.roll(b_in, d, axis=0)
        b_in = jnp.where(valid, a * b_sh, 0.0) + b_in
        a = jnp.where(valid, a * a_sh, a)
        d *= 2
    while d < t:
        b_in = jnp.concatenate([b_in[:d], a[d:] * b_in[:t - d] + b_in[d:]], axis=0)
        a = jnp.concatenate([a[:d], a[d:] * a[:t - d]], axis=0)
        d *= 2
    h = a * hprev[...] + b_in
    hprev[...] = h[t - 1:t, :]

    ya = h * jax.nn.gelu(gr_ref[...].astype(F32))
    out_ref[...] = (_sigmoid(ga_ref[...].astype(F32)) * ya).astype(BF16)


def _rope_rows(x, cos, sin):
    half = QK_ROPE // 2
    x1, x2 = x[:, :half, :], x[:, half:, :]
    return x1 * cos - x2 * sin, x1 * sin + x2 * cos


def _mla_prep_kernel(qc_ref, kv_ref, pe_ref, pos_ref, freq_ref, gq_ref, gkv_ref, qg_ref, kg_ref,
                     wq_ref, wk_ref, wv_ref, q_out, k_out, v_out, qn_out, kn_out):
    t = TOK_TILE
    ang = pos_ref[...].astype(F32) * freq_ref[...]
    cos, sin = jnp.cos(ang), jnp.sin(ang)

    qct = qc_ref[...].astype(F32).T
    qcn = qct * lax.rsqrt(jnp.mean(qct * qct, axis=0, keepdims=True) + EPS) * gq_ref[...]
    q = jnp.dot(wq_ref[...], qcn.astype(BF16), preferred_element_type=F32)
    q = q.reshape(HEADS, QK_HEAD, t)
    q = q * lax.rsqrt(jnp.mean(q * q, axis=1, keepdims=True) + EPS) * qg_ref[...][None]
    r1, r2 = _rope_rows(q[:, QK_NOPE:, :], cos, sin)
    scale = (QK_HEAD ** -0.5) * LOG2E
    q_bf = (jnp.concatenate([q[:, :QK_NOPE, :], r1, r2], axis=1) * scale).astype(BF16)
    q_out[...] = jnp.concatenate([q_bf, jnp.zeros((HEADS, QK_PAD - QK_HEAD, t), BF16)], axis=1)
    q_f = q_bf.astype(F32)
    qn_out[...] = jnp.sqrt(jnp.sum(q_f * q_f, axis=1, keepdims=True))

    kvt = kv_ref[...].astype(F32).T
    kvn = (kvt * lax.rsqrt(jnp.mean(kvt * kvt, axis=0, keepdims=True) + EPS) * gkv_ref[...]).astype(BF16)
    kn = jnp.dot(wk_ref[...], kvn, preferred_element_type=F32).reshape(HEADS, QK_NOPE, t)
    v = jnp.dot(wv_ref[...], kvn, preferred_element_type=F32).reshape(HEADS, V_HEAD, t)
    pe = pe_ref[...].T[:QK_ROPE, :]
    ssq = jnp.sum(kn * kn, axis=1, keepdims=True) + jnp.sum(pe * pe, axis=0, keepdims=True)[None]
    rstd = lax.rsqrt(ssq * (1.0 / QK_HEAD) + EPS)
    kg = kg_ref[...]
    k_nope = kn * rstd * kg[None, :QK_NOPE, :]
    k_pe = (pe * kg[QK_NOPE:, :])[None] * rstd
    p1, p2 = _rope_rows(k_pe, cos, sin)
    k_real = jnp.concatenate([k_nope, p1, p2], axis=1)
    pad = jnp.where(lax.broadcasted_iota(jnp.int32, (HEADS, QK_PAD - QK_HEAD, t), 1) == 0, 1.0, 0.0)
    kt = jnp.concatenate([k_real, pad], axis=1)
    for h in range(HEADS):
        k_out[h] = kt[h].T.astype(BF16)
    k_f = k_real.astype(BF16).astype(F32)
    k_max = jnp.max(jnp.sum(k_f * k_f, axis=1, keepdims=True), axis=2, keepdims=True)
    kn_out[...] = jnp.broadcast_to(jnp.sqrt(k_max), (HEADS, 1, LANES))
    v_out[...] = jnp.concatenate([v, jnp.ones((HEADS, V_AUG - V_HEAD, t), F32)], axis=1).astype(BF16)


def _token_mix_prep_kernel(x_ref, g_ref, w_ref, cw_ref, cb_ref, sh_ref, wa_ref, ba_ref, wx_ref, bx_ref, lam_ref,
                           pos_ref, freq_ref, gq_ref, gkv_ref, qg_ref, kg_ref, wq_ref, wk_ref, wv_ref,
                           ya_ref, gl_ref, q_out, k_out, v_out, qn_out, kn_out,
                           xr_s, gr_s, qc_s, kv_s, pe_s, xprev, hprev):
    _inproj_kernel(x_ref, g_ref, w_ref, xr_s, gr_s, qc_s, kv_s, pe_s, gl_ref)
    _lru_kernel(xr_s, gr_s, gl_ref.at[:, pl.ds(0, D_MODEL)], cw_ref, cb_ref, sh_ref, wa_ref, ba_ref, wx_ref, bx_ref,
                lam_ref, ya_ref, xprev, hprev)
    _mla_prep_kernel(qc_s, kv_s, pe_s, pos_ref, freq_ref, gq_ref, gkv_ref, qg_ref, kg_ref, wq_ref, wk_ref, wv_ref,
                     q_out, k_out, v_out, qn_out, kn_out)


def _token_mix_prep(x2, g, w_cat, conv_w, conv_b, wa, ba, wx, bx, lam, pos_row, freq, gq, gkv, qg, kg,
                    wq_t, wk_t, wv_t):
    s = x2.shape[0]
    t = TOK_TILE
    nt = s // t
    row = lambda n: pl.BlockSpec((t, n), lambda i: (i, 0))
    blk = lambda r, c: pl.BlockSpec((HEADS, None, r, c), lambda i: (0, i, 0, 0))
    lru_w = _const_spec((LRU_BLOCKS, LRU_BLOCK_W, LRU_BLOCK_W))
    shifts = jnp.stack([jnp.eye(t, k=-(CONV_W - 1 - j), dtype=BF16) for j in range(CONV_W - 1)])
    return pl.pallas_call(
        _token_mix_prep_kernel,
        grid=(nt,),
        in_specs=[row(D_MODEL), _const_spec((1, D_MODEL)),
                  pl.BlockSpec((D_MODEL, _N_IN), lambda i: (0, 0), pipeline_mode=pl.Buffered(1)),
                  _const_spec((CONV_W, D_RNN)), _const_spec((1, D_RNN)), _const_spec((CONV_W - 1, t, t)),
                  lru_w, _const_spec((1, D_RNN)), lru_w, _const_spec((1, D_RNN)), _const_spec((1, D_RNN)),
                  pl.BlockSpec((None, 1, t), lambda i: (i, 0, 0)),
                  _const_spec((QK_ROPE // 2, 1)), _const_spec((Q_LORA, 1)), _const_spec((KV_LORA, 1)),
                  _const_spec((QK_HEAD, 1)), _const_spec((QK_HEAD, 1)),
                  _const_spec((HEADS * QK_HEAD, Q_LORA)), _const_spec((HEADS * QK_NOPE, KV_LORA)),
                  _const_spec((HEADS * V_HEAD, KV_LORA))],
        out_specs=[row(D_RNN), row(2 * D_MODEL),
                   blk(QK_PAD, t), blk(t, QK_PAD), blk(V_AUG, t), blk(1, t), blk(1, LANES)],
        out_shape=[jax.ShapeDtypeStruct((s, D_RNN), BF16), jax.ShapeDtypeStruct((s, 2 * D_MODEL), BF16),
                   jax.ShapeDtypeStruct((HEADS, nt, QK_PAD, t), BF16),
                   jax.ShapeDtypeStruct((HEADS, nt, t, QK_PAD), BF16),
                   jax.ShapeDtypeStruct((HEADS, nt, V_AUG, t), BF16),
                   jax.ShapeDtypeStruct((HEADS, nt, 1, t), F32),
                   jax.ShapeDtypeStruct((HEADS, nt, 1, LANES), F32)],
        scratch_shapes=[pltpu.VMEM((t, D_RNN), BF16), pltpu.VMEM((t, D_RNN), BF16), pltpu.VMEM((t, Q_LORA), BF16),
                        pltpu.VMEM((t, KV_LORA), BF16), pltpu.VMEM((t, LANES), F32),
                        pltpu.VMEM((8, D_RNN), F32), pltpu.VMEM((1, D_RNN), F32)],
        compiler_params=_cparams("arbitrary"),
        name="token_mix_prep",
    )(x2, g, w_cat, conv_w, conv_b, shifts, wa, ba, wx, bx, lam, pos_row, freq, gq, gkv, qg, kg, wq_t, wk_t, wv_t)


def _attn_kernel(fixed_ref, *refs):
    if fixed_ref:
        q_ref, k_ref, v_ref, qn_ref, kn_ref, o_ref, p_scr = refs
    else:
        q_ref, k_ref, v_ref, o_ref, s_scr = refs
    tq, tk = ATT_TQ, TOK_TILE
    nsub = tq // tk
    qi = pl.program_id(1)

    def query_block(g, a):
        q = q_ref[g, a]
        if not fixed_ref:
            return q
        k_max = jnp.max(kn_ref[g], axis=0)[:, :1]
        ref = qn_ref[g, a] * k_max
        row = lax.broadcasted_iota(jnp.int32, (QK_PAD - QK_HEAD, tk), 0)
        extra = jnp.where(row == 0, -ref, 0.0).astype(BF16)
        return jnp.concatenate([q[:QK_HEAD, :], extra], axis=0)

    qs = [jnp.concatenate([query_block(g, a) for a in range(nsub)], axis=1) for g in range(ATT_HEADS)]

    def finish(accs):
        for g in range(ATT_HEADS):
            o = accs[g][:V_HEAD, :] / accs[g][V_HEAD:V_HEAD + 1, :]
            for a in range(nsub):
                o_ref[g, a] = o[:, a * tk:(a + 1) * tk]

    if fixed_ref:
        chunk_gap = ((lax.broadcasted_iota(jnp.int32, (tk, tq), 0) >> CHUNK_SHIFT)
                     - (lax.broadcasted_iota(jnp.int32, (tk, tq), 1) >> CHUNK_SHIFT))

        def produce(kb, slot):
            lead = qi * (tq >> CHUNK_SHIFT) - kb * (tk >> CHUNK_SHIFT)
            for g in range(ATT_HEADS):
                s = jnp.dot(k_ref[g, kb], qs[g], preferred_element_type=F32)
                p_scr[slot, g] = jnp.exp2(jnp.where(chunk_gap <= lead, s, NEG_BIG)).astype(BF16)

        def accumulate(kb, slot, accs):
            return tuple(accs[g] + jnp.dot(v_ref[g, kb], p_scr[slot, g], preferred_element_type=F32)
                         for g in range(ATT_HEADS))

        def pair(i, accs):
            produce(2 * i + 1, 1)
            accs = accumulate(2 * i, 0, accs)
            produce(2 * i + 2, 0)
            return accumulate(2 * i + 1, 1, accs)

        assert nsub % 2 == 0
        n_pairs = (qi + 1) * (nsub // 2)
        accs = tuple(jnp.zeros((V_AUG, tq), F32) for _ in range(ATT_HEADS))
        produce(0, 0)
        accs = lax.fori_loop(0, n_pairs - 1, pair, accs)
        kb = 2 * (n_pairs - 1)
        produce(kb + 1, 1)
        finish(accumulate(kb + 1, 1, accumulate(kb, 0, accs)))
        return

    def prefetch(kb, slot):
        for g in range(ATT_HEADS):
            s_scr[slot, g] = jnp.dot(k_ref[g, kb], qs[g], preferred_element_type=F32)

    def consume(kb, slot, state, tail):
        out = []
        for g in range(ATT_HEADS):
            m, acc = state[g]
            if tail is None:
                visible = None
            else:
                kk = (lax.broadcasted_iota(jnp.int32, (tk, tq), 0) + tail * tk) >> CHUNK_SHIFT
                qq = lax.broadcasted_iota(jnp.int32, (tk, tq), 1) >> CHUNK_SHIFT
                visible = kk <= qq
            load = lambda: s_scr[slot, g] if visible is None else jnp.where(visible, s_scr[slot, g], NEG_BIG)
            m_new = jnp.maximum(m, jnp.max(load(), axis=0, keepdims=True))
            p = jnp.exp2(load() - m_new).astype(BF16)
            acc = acc * jnp.exp2(m - m_new) + jnp.dot(v_ref[g, kb], p, preferred_element_type=F32)
            out.append((m_new, acc))
        return tuple(out)

    def pair(i, state):
        for t in range(2):
            prefetch(2 * i + t + 1, (t + 1) % 2)
            state = consume(2 * i + t, t, state, None)
        return state

    assert nsub % 2 == 0
    state = tuple((jnp.full((1, tq), NEG_BIG, F32), jnp.zeros((V_AUG, tq), F32)) for _ in range(ATT_HEADS))
    prefetch(0, 0)
    state = lax.fori_loop(0, qi * (nsub // 2), pair, state)
    kb = qi * nsub
    for t in range(nsub):
        if t + 1 < nsub:
            prefetch(kb + t + 1, (t + 1) % 2)
        state = consume(kb + t, t % 2, state, t)
    finish(tuple(acc for _, acc in state))


def _attention(q_t, k, v_t, q_norm, k_norm, fixed_ref):
    heads, nt = q_t.shape[0], q_t.shape[1]
    s = nt * TOK_TILE
    assert s % ATT_TQ == 0 and heads % ATT_HEADS == 0
    qsub = ATT_TQ // TOK_TILE
    resident = dict(pipeline_mode=pl.Buffered(1))
    in_specs = [pl.BlockSpec((ATT_HEADS, qsub, QK_PAD, TOK_TILE), lambda h, i: (h, i, 0, 0)),
                pl.BlockSpec((ATT_HEADS, nt, TOK_TILE, QK_PAD), lambda h, i: (h, 0, 0, 0), **resident),
                pl.BlockSpec((ATT_HEADS, nt, V_AUG, TOK_TILE), lambda h, i: (h, 0, 0, 0), **resident)]
    args = (q_t, k, v_t)
    scratch = [pltpu.VMEM((2, ATT_HEADS, TOK_TILE, ATT_TQ), F32)]
    if fixed_ref:
        in_specs += [pl.BlockSpec((ATT_HEADS, qsub, 1, TOK_TILE), lambda h, i: (h, i, 0, 0)),
                     pl.BlockSpec((ATT_HEADS, nt, 1, LANES), lambda h, i: (h, 0, 0, 0))]
        args += (q_norm, k_norm)
        scratch = [pltpu.VMEM((2, ATT_HEADS, TOK_TILE, ATT_TQ), BF16)]
    return pl.pallas_call(
        functools.partial(_attn_kernel, fixed_ref),
        grid=(heads // ATT_HEADS, s // ATT_TQ),
        in_specs=in_specs,
        out_specs=pl.BlockSpec((ATT_HEADS, qsub, V_HEAD, TOK_TILE), lambda h, i: (h, i, 0, 0)),
        out_shape=jax.ShapeDtypeStruct((heads, nt, V_HEAD, TOK_TILE), F32),
        scratch_shapes=scratch,
        compiler_params=_cparams("arbitrary", "arbitrary"),
        name="attention_fixed_ref" if fixed_ref else "attention_online",
    )(*args)


def _first_lane(cond, lane):
    return jnp.min(jnp.where(cond, lane.astype(F32), 4.0 * LANES), axis=1, keepdims=True).astype(jnp.int32)


def _outproj_router_kernel(x_ref, ya_ref, gb_ref, yb_ref, wo_ref, gf_ref, wrh_ref, wrl_ref, br_ref,
                           h_ref, xn_ref, ri_ref, rf_ref, cnt_ref, base):
    t = TOK_TILE

    @pl.when(pl.program_id(0) == 0)
    def _():
        base[...] = jnp.zeros((1, LANES), F32)

    yb = yb_ref[...].reshape(HEADS * V_HEAD, t).T
    merged = ya_ref[...].astype(F32) + _sigmoid(gb_ref[...].astype(F32)) * yb
    h = x_ref[...] + jnp.dot(merged.astype(BF16), wo_ref[...], preferred_element_type=F32)
    h_ref[...] = h
    xn = h * lax.rsqrt(jnp.mean(h * h, axis=-1, keepdims=True) + EPS) * gf_ref[...]
    xn_ref[...] = xn

    x_hi = xn.astype(BF16)
    x_lo = (xn - x_hi.astype(F32)).astype(BF16)
    logits = (jnp.dot(x_hi, wrh_ref[...], preferred_element_type=F32)
              + jnp.dot(x_hi, wrl_ref[...], preferred_element_type=F32)
              + jnp.dot(x_lo, wrh_ref[...], preferred_element_type=F32)) + br_ref[...]

    lane = lax.broadcasted_iota(jnp.int32, (t, LANES), 1)
    is_grp = jnp.logical_and(lane >= N_EXPERTS, lane < N_EXPERTS + N_GROUPS)
    gl = jnp.where(is_grp, logits, NEG_BIG)
    gmax = jnp.max(gl, axis=1, keepdims=True)
    grp = _first_lane(gl == gmax, lane) - N_EXPERTS
    p_grp = 1.0 / jnp.sum(jnp.where(is_grp, jnp.exp(gl - gmax), 0.0), axis=1, keepdims=True)
    el = jnp.where((lane >> GROUP_SHIFT) == grp, logits, NEG_BIG)
    v1 = jnp.max(el, axis=1, keepdims=True)
    e1 = _first_lane(el == v1, lane)
    el2 = jnp.where(lane == e1, NEG_BIG, el)
    v2 = jnp.max(el2, axis=1, keepdims=True)
    e2 = _first_lane(el2 == v2, lane)
    ex = jnp.exp(v2 - v1)
    w1 = p_grp / (1.0 + ex)
    w2 = p_grp * ex / (1.0 + ex)

    tri = (lax.broadcasted_iota(jnp.int32, (t, t), 1) < lax.broadcasted_iota(jnp.int32, (t, t), 0))
    tri = jnp.where(tri, 1.0, 0.0).astype(BF16)
    oh1 = jnp.where(lane == e1, 1.0, 0.0)
    oh2 = jnp.where(lane == e2, 1.0, 0.0)
    pre1 = jnp.dot(tri, oh1.astype(BF16), preferred_element_type=F32) + base[...]
    base1 = base[...] + jnp.sum(oh1, axis=0, keepdims=True)
    pre2 = jnp.dot(tri, oh2.astype(BF16), preferred_element_type=F32) + base1
    base2 = base1 + jnp.sum(oh2, axis=0, keepdims=True)
    r1 = jnp.sum(oh1 * pre1, axis=1, keepdims=True).astype(jnp.int32)
    r2 = jnp.sum(oh2 * pre2, axis=1, keepdims=True).astype(jnp.int32)
    base[...] = base2
    cnt_ref[...] = base2

    ri = jnp.where(lane == 0, e1, jnp.where(lane == 1, e2, jnp.where(lane == 2, r1, r2)))
    ri_ref[...] = ri.astype(F32).T[:8, :].astype(jnp.int32)
    rf_ref[...] = jnp.where(lane == 0, w1, w2)


def _outproj_router(x2, ya_g, gl, yb_t, w_out, gf, wr_hi, wr_lo, br):
    s = x2.shape[0]
    t = TOK_TILE
    row = lambda n: pl.BlockSpec((t, n), lambda i: (i, 0))
    return pl.pallas_call(
        _outproj_router_kernel,
        grid=(s // t,),
        in_specs=[row(D_MODEL), row(D_MODEL), pl.BlockSpec((t, D_MODEL), lambda i: (i, 1)),
                  pl.BlockSpec((HEADS, None, V_HEAD, t), lambda i: (0, i, 0, 0)),
                  _const_spec((D_MODEL, D_MODEL)), _const_spec((1, D_MODEL)),
                  _const_spec((D_MODEL, LANES)), _const_spec((D_MODEL, LANES)), _const_spec((1, LANES))],
        out_specs=[row(D_MODEL), row(D_MODEL), pl.BlockSpec((None, 8, t), lambda i: (i, 0, 0)), row(LANES),
                   _const_spec((1, LANES))],
        out_shape=[jax.ShapeDtypeStruct((s, D_MODEL), F32), jax.ShapeDtypeStruct((s, D_MODEL), F32),
                   jax.ShapeDtypeStruct((s // t, 8, t), jnp.int32), jax.ShapeDtypeStruct((s, LANES), F32),
                   jax.ShapeDtypeStruct((1, LANES), F32)],
        scratch_shapes=[pltpu.VMEM((1, LANES), F32)],
        compiler_params=_cparams("arbitrary"),
        name="outproj_router",
    )(x2, ya_g, gl, yb_t, w_out, gf, wr_hi, wr_lo, br)


def _row_copy(src_ref, src_row, dst_ref, dst_row, sem):
    return pltpu.make_async_copy(src_ref.at[pl.ds(src_row, 1), :], dst_ref.at[pl.ds(dst_row, 1), :], sem)


def _slot(ps_ref, ri_ref, k, i):
    return ps_ref[ri_ref[k, i]] + ri_ref[2 + k, i]


def _scatter_kernel(ps_ref, ri_ref, xn_ref, zeros_ref, xpad_ref, sem):
    del zeros_ref
    t = TOK_TILE

    def issue(i, c):
        for k in range(2):
            _row_copy(xn_ref, i, xpad_ref, _slot(ps_ref, ri_ref, k, i), sem).start(priority=k)
        return c

    lax.fori_loop(0, t, issue, 0, unroll=DMA_UNROLL)

    def drain(i, c):
        for _ in range(2):
            _row_copy(xn_ref, 0, xpad_ref, 0, sem).wait()
        return c

    lax.fori_loop(0, t, drain, 0, unroll=DMA_UNROLL)


def _scatter_rows(p_start, ri_t, xn, xpad_zeros):
    s = xn.shape[0]
    t = TOK_TILE
    grid_spec = pltpu.PrefetchScalarGridSpec(
        num_scalar_prefetch=1,
        grid=(s // t,),
        in_specs=[pl.BlockSpec((None, 8, t), lambda i, ps: (i, 0, 0), memory_space=pltpu.SMEM),
                  pl.BlockSpec((t, D_MODEL), lambda i, ps: (i, 0)), pl.BlockSpec(memory_space=pl.ANY)],
        out_specs=pl.BlockSpec(memory_space=pl.ANY),
        scratch_shapes=[pltpu.SemaphoreType.DMA(())],
    )
    return pl.pallas_call(
        _scatter_kernel,
        grid_spec=grid_spec,
        out_shape=jax.ShapeDtypeStruct(xpad_zeros.shape, F32),
        input_output_aliases={3: 0},
        compiler_params=_cparams("arbitrary"),
        name="moe_scatter",
    )(p_start, ri_t, xn, xpad_zeros)


def _expert_kernel(be_ref, na_ref, x_ref, wg_ref, wu_ref, wd_ref, y_ref, wgu_bf, wd_bf):
    b = pl.program_id(0)
    active = b < na_ref[0]

    @pl.when(jnp.logical_and(active, jnp.logical_or(b == 0, be_ref[b] != be_ref[jnp.maximum(b - 1, 0)])))
    def _():
        wgu_bf[:, :D_EXPERT] = wg_ref[...].astype(BF16)
        wgu_bf[:, D_EXPERT:] = wu_ref[...].astype(BF16)
        wd_bf[...] = wd_ref[...].astype(BF16)

    @pl.when(active)
    def _():
        x = x_ref[...].astype(BF16)
        gu = jnp.dot(x, wgu_bf[...], preferred_element_type=F32)
        g, u = gu[:, :D_EXPERT], gu[:, D_EXPERT:]
        hid = (g * _sigmoid(g) * u).astype(BF16)
        y_ref[...] = jnp.dot(hid, wd_bf[...], preferred_element_type=F32)

    @pl.when(jnp.logical_not(active))
    def _():
        y_ref[...] = jnp.zeros_like(y_ref)


def _experts(blk_expert, n_active, xpad, w_gate, w_up, w_down):
    nblk = xpad.shape[0] // MOE_BLOCK
    rows = lambda b, be, na: (jnp.minimum(b, na[0] - 1), 0)
    expert = lambda b, be, na: (be[b], 0, 0)
    grid_spec = pltpu.PrefetchScalarGridSpec(
        num_scalar_prefetch=2,
        grid=(nblk,),
        in_specs=[pl.BlockSpec((MOE_BLOCK, D_MODEL), rows),
                  pl.BlockSpec((None, D_MODEL, D_EXPERT), expert), pl.BlockSpec((None, D_MODEL, D_EXPERT), expert),
                  pl.BlockSpec((None, D_EXPERT, D_MODEL), expert)],
        out_specs=pl.BlockSpec((MOE_BLOCK, D_MODEL), lambda b, be, na: (b, 0)),
        scratch_shapes=[pltpu.VMEM((D_MODEL, 2 * D_EXPERT), BF16), pltpu.VMEM((D_EXPERT, D_MODEL), BF16)],
    )
    return pl.pallas_call(
        _expert_kernel,
        grid_spec=grid_spec,
        out_shape=jax.ShapeDtypeStruct(xpad.shape, F32),
        compiler_params=_cparams("arbitrary"),
        name="moe_experts",
    )(blk_expert, n_active, xpad, w_gate, w_up, w_down)


def _combine_kernel(ps_ref, ri_ref, rin_ref, h_ref, rf_ref, ypad_ref, out_ref, ybuf, sem):
    t = TOK_TILE
    step = pl.program_id(0)
    cur = step % 2

    def fetch(idx_ref, slot):
        def issue(i, c):
            for k in range(2):
                _row_copy(ypad_ref, _slot(ps_ref, idx_ref, k, i), ybuf.at[slot, k], i,
                          sem.at[slot]).start(priority=k)
            return c

        lax.fori_loop(0, t, issue, 0, unroll=DMA_UNROLL)

    @pl.when(step == 0)
    def _():
        fetch(ri_ref, 0)

    @pl.when(step + 1 < pl.num_programs(0))
    def _():
        fetch(rin_ref, 1 - cur)

    def drain(i, c):
        for k in range(2):
            _row_copy(ypad_ref, 0, ybuf.at[cur, k], 0, sem.at[cur]).wait()
        return c

    lax.fori_loop(0, t, drain, 0, unroll=DMA_UNROLL)
    rf = rf_ref[...]
    out_ref[...] = h_ref[...] + rf[:, 0:1] * ybuf[cur, 0] + rf[:, 1:2] * ybuf[cur, 1]


def _combine(p_start, ri_t, h, rf, ypad):
    s = h.shape[0]
    t = TOK_TILE
    nt = s // t
    idx = lambda shift: pl.BlockSpec((None, 8, t), lambda i, ps: (jnp.minimum(i + shift, nt - 1), 0, 0),
                                     memory_space=pltpu.SMEM)
    grid_spec = pltpu.PrefetchScalarGridSpec(
        num_scalar_prefetch=1,
        grid=(nt,),
        in_specs=[idx(0), idx(1), pl.BlockSpec((t, D_MODEL), lambda i, ps: (i, 0)),
                  pl.BlockSpec((t, LANES), lambda i, ps: (i, 0)), pl.BlockSpec(memory_space=pl.ANY)],
        out_specs=pl.BlockSpec((t, D_MODEL), lambda i, ps: (i, 0)),
        scratch_shapes=[pltpu.VMEM((2, 2, t, D_MODEL), F32), pltpu.SemaphoreType.DMA((2,))],
    )
    return pl.pallas_call(
        _combine_kernel,
        grid_spec=grid_spec,
        out_shape=jax.ShapeDtypeStruct((s, D_MODEL), F32),
        compiler_params=_cparams("arbitrary"),
        name="moe_combine",
    )(p_start, ri_t, ri_t, h, rf, ypad)


def _layer(h3, positions, norm_mix_g, w_in, conv_w, conv_b, lru_wa, lru_ba, lru_wx, lru_bx, lru_lambda,
           q_a_g, w_uq, kv_a_g, w_ukv, q_norm_g, k_norm_g, w_out, norm_ffn_g,
           router_group_w, router_group_b, router_expert_w, router_expert_b, w_gate, w_up, w_down):
    b, s, _ = h3.shape
    assert b == 1 and s % TOK_TILE == 0
    x2 = h3.reshape(s, D_MODEL)
    nt = s // TOK_TILE

    o = np.cumsum((0, D_RNN, D_RNN, Q_LORA, KV_LORA, QK_ROPE))
    w_cat = jnp.concatenate(
        [w_in[:, o[0]:o[4]], w_in[:, o[4]:o[5]], jnp.zeros((D_MODEL, LANES - QK_ROPE), F32), w_in[:, o[5]:]],
        axis=1).astype(BF16)
    row = lambda v: v.reshape(1, -1)
    col = lambda v: v.reshape(-1, 1)
    wq_t = w_uq.T.astype(BF16)
    ukv = w_ukv.reshape(KV_LORA, HEADS, QK_NOPE + V_HEAD)
    wk_t = ukv[:, :, :QK_NOPE].reshape(KV_LORA, HEADS * QK_NOPE).T.astype(BF16)
    wv_t = ukv[:, :, QK_NOPE:].reshape(KV_LORA, HEADS * V_HEAD).T.astype(BF16)
    half = QK_ROPE // 2
    freq = col(ROPE_THETA ** (-jnp.arange(half, dtype=F32) / half))
    wr = jnp.concatenate([router_expert_w, router_group_w,
                          jnp.zeros((D_MODEL, LANES - N_EXPERTS - N_GROUPS), F32)], axis=1)
    wr_hi = wr.astype(BF16)
    wr_lo = (wr - wr_hi.astype(F32)).astype(BF16)
    br = row(jnp.concatenate([router_expert_b, router_group_b, jnp.zeros((LANES - N_EXPERTS - N_GROUPS,), F32)]))

    ya_g, gl, q_t, k_r, v_t, q_norm, k_norm = _token_mix_prep(
        x2, row(norm_mix_g), w_cat, conv_w, row(conv_b), lru_wa.astype(BF16), row(lru_ba), lru_wx.astype(BF16),
        row(lru_bx), row(lru_lambda), positions.reshape(nt, 1, TOK_TILE), freq, col(q_a_g), col(kv_a_g),
        col(q_norm_g), col(k_norm_g), wq_t, wk_t, wv_t)
    bound = jnp.max(jnp.max(q_norm, axis=(1, 2, 3)) * jnp.max(k_norm, axis=(1, 2, 3)))
    yb_t = lax.cond(bound <= ATT_FIXED_REF_LIMIT,
                    lambda: _attention(q_t, k_r, v_t, q_norm, k_norm, True),
                    lambda: _attention(q_t, k_r, v_t, q_norm, k_norm, False))
    h, xn, ri, rf, counts = _outproj_router(x2, ya_g, gl, yb_t, w_out.astype(BF16), row(norm_ffn_g),
                                            wr_hi, wr_lo, br)

    cnt = counts[0, :N_EXPERTS].astype(jnp.int32)
    padded = (cnt + MOE_BLOCK - 1) // MOE_BLOCK * MOE_BLOCK
    p_end = jnp.cumsum(padded)
    p_start = p_end - padded
    n_slots = 2 * s + N_EXPERTS * MOE_BLOCK
    nblk = n_slots // MOE_BLOCK
    blk_first = jnp.arange(nblk, dtype=jnp.int32) * MOE_BLOCK
    blk_expert = jnp.minimum(jnp.sum((p_end[None, :] <= blk_first[:, None]).astype(jnp.int32), axis=1),
                             N_EXPERTS - 1)
    n_active = (p_end[-1:] // MOE_BLOCK).astype(jnp.int32)
    p_start = p_start.astype(jnp.int32)

    xpad = _scatter_rows(p_start, ri, xn, jnp.zeros((n_slots, D_MODEL), F32))
    ypad = _experts(blk_expert, n_active, xpad, w_gate, w_up, w_down)
    out = _combine(p_start, ri, h, rf, ypad)
    return out.reshape(b, s, D_MODEL)


def kernel(x, positions, norm_mix_g, w_in, conv_w, conv_b, lru_wa, lru_ba, lru_wx, lru_bx, lru_lambda, q_a_g,
           w_uq, kv_a_g, w_ukv, q_norm_g, k_norm_g, w_out, norm_ffn_g, router_group_w, router_group_b,
           router_expert_w, router_expert_b, w_gate, w_up, w_down):
    h = x
    for l in range(norm_mix_g.shape[0]):
        h = _layer(h, positions, norm_mix_g[l], w_in[l], conv_w[l], conv_b[l], lru_wa[l], lru_ba[l], lru_wx[l],
                   lru_bx[l], lru_lambda[l], q_a_g[l], w_uq[l], kv_a_g[l], w_ukv[l], q_norm_g[l], k_norm_g[l],
                   w_out[l], norm_ffn_g[l], router_group_w[l], router_group_b[l], router_expert_w[l],
                   router_expert_b[l], w_gate[l], w_up[l], w_down[l])
    return h
```

```python
import functools

import jax
import jax.numpy as jnp
import numpy as np
from jax import lax
from jax.experimental import pallas as pl
from jax.experimental.pallas import tpu as pltpu

F32 = jnp.float32
BF16 = jnp.bfloat16

D_MODEL = 1024
EPS = 1e-6
CHUNK = 64
D_RNN = 1024
LRU_BLOCKS = 4
LRU_BLOCK_W = 256
CONV_W = 4
LRU_C = 8.0
HEADS = 16
V_HEAD = 64
QK_NOPE = 64
QK_ROPE = 32
QK_HEAD = 96
Q_LORA = 768
KV_LORA = 256
ROPE_THETA = 10000.0
N_GROUPS = 4
EXPERTS_PER_GROUP = 8
N_EXPERTS = 32
D_EXPERT = 256
CHUNK_SHIFT = CHUNK.bit_length() - 1
GROUP_SHIFT = EXPERTS_PER_GROUP.bit_length() - 1

LANES = 128
TOK_TILE = 256
ATT_TQ = 512
ATT_HEADS = 4
ATT_FIXED_REF_LIMIT = 48.0
QK_PAD = 128
V_AUG = 80
MOE_BLOCK = 256
DMA_UNROLL = 8
NEG_BIG = -1e30
LOG2E = 1.4426950408889634
VMEM_LIMIT = 56 * 1024 * 1024

_SEG_XR, _SEG_GR, _SEG_QC, _SEG_KV, _SEG_PE, _SEG_GL = 0, 1024, 2048, 2816, 3072, 3200
_N_IN = 5248


def _cparams(*sem):
    return pltpu.CompilerParams(dimension_semantics=sem, vmem_limit_bytes=VMEM_LIMIT)


def _const_spec(shape):
    nd = len(shape)
    return pl.BlockSpec(shape, lambda *_: (0,) * nd)


ROW_TILE = D_MODEL // LANES


def _to_row_tiles(dst_ref, val):
    n = val.shape[0]
    for c in range(ROW_TILE):
        dst_ref[pl.ds(c, n, stride=ROW_TILE), :] = val[:, c * LANES:(c + 1) * LANES]


def _from_row_tiles(src_ref, n):
    return jnp.concatenate([src_ref[pl.ds(c, n, stride=ROW_TILE), :] for c in range(ROW_TILE)], axis=1)


def _inproj_kernel(x_ref, g_ref, w_ref, xr_ref, gr_ref, qc_ref, kv_ref, pe_ref, gl_ref):
    x = x_ref[...]
    ms = jnp.mean(x * x, axis=-1, keepdims=True)
    u = (x * lax.rsqrt(ms + EPS) * g_ref[...]).astype(BF16)
    p = jnp.dot(u, w_ref[...], preferred_element_type=F32)
    xr_ref[...] = p[:, _SEG_XR:_SEG_GR].astype(BF16)
    gr_ref[...] = p[:, _SEG_GR:_SEG_QC].astype(BF16)
    qc_ref[...] = p[:, _SEG_QC:_SEG_KV].astype(BF16)
    kv_ref[...] = p[:, _SEG_KV:_SEG_PE].astype(BF16)
    pe_ref[...] = p[:, _SEG_PE:_SEG_GL]
    gl_ref[...] = p[:, _SEG_GL:_N_IN].astype(BF16)


def _sigmoid(x):
    return 1.0 / (1.0 + jnp.exp(-x))


def _lru_kernel(xr_ref, gr_ref, ga_ref, cw_ref, cb_ref, sh_ref, wa_ref, ba_ref, wx_ref, bx_ref, lam_ref,
                out_ref, xprev, hprev):
    t = xr_ref.shape[0]

    x_bf = xr_ref[...]
    x = x_bf.astype(F32)
    xa = cb_ref[...] + cw_ref[CONV_W - 1:CONV_W, :] * x
    top = jnp.zeros((8, D_RNN), F32)
    row8 = lax.broadcasted_iota(jnp.int32, (8, D_RNN), 0)
    for j in range(CONV_W - 1):
        shift = CONV_W - 1 - j
        xa = xa + cw_ref[j:j + 1, :] * jnp.dot(sh_ref[j], x_bf, preferred_element_type=F32)
        top = top + cw_ref[j:j + 1, :] * jnp.where(row8 < shift, pltpu.roll(xprev[...], shift, axis=0), 0.0)
    xa = jnp.concatenate([xa[:8, :] + top, xa[8:, :]], axis=0)
    xprev[...] = x[t - 8:t, :]

    r_parts, i_parts = [], []
    for b in range(LRU_BLOCKS):
        xb = xa[:, b * LRU_BLOCK_W:(b + 1) * LRU_BLOCK_W].astype(BF16)
        r_parts.append(jnp.dot(xb, wa_ref[b], preferred_element_type=F32))
        i_parts.append(jnp.dot(xb, wx_ref[b], preferred_element_type=F32))
    r = _sigmoid(jnp.concatenate(r_parts, axis=1) + ba_ref[...])
    gi = _sigmoid(jnp.concatenate(i_parts, axis=1) + bx_ref[...])
    z = -lam_ref[...]
    softplus = jnp.maximum(z, 0.0) + jnp.log(1.0 + jnp.exp(-jnp.abs(z)))
    log_a = (-LRU_C) * r * softplus
    a = jnp.exp(log_a)
    b_in = jnp.sqrt(1.0 - a * a) * (gi * xa)

    row = lax.broadcasted_iota(jnp.int32, (t, D_RNN), 0)
    d = 1
    while d < 8:
        valid = row >= d
        a_sh = pltpu.roll(a, d, axis=0)
        b_sh = pltpu.roll(b_in, d, axis=0)
        b_in = jnp.where(valid, a * b_sh, 0.0) + b_in
        a = jnp.where(valid, a * a_sh, a)
        d *= 2
    while d < t:
        b_in = jnp.concatenate([b_in[:d], a[d:] * b_in[:t - d] + b_in[d:]], axis=0)
        a = jnp.concatenate([a[:d], a[d:] * a[:t - d]], axis=0)
        d *= 2
    h = a * hprev[...] + b_in
    hprev[...] = h[t - 1:t, :]

    ya = h * jax.nn.gelu(gr_ref[...].astype(F32))
    out_ref[...] = (_sigmoid(ga_ref[...].astype(F32)) * ya).astype(BF16)


def _rope_rows(x, cos, sin):
    half = QK_ROPE // 2
    x1, x2 = x[:, :half, :], x[:, half:, :]
    return x1 * cos - x2 * sin, x1 * sin + x2 * cos


def _mla_prep_kernel(qc_ref, kv_ref, pe_ref, pos_ref, freq_ref, gq_ref, gkv_ref, qg_ref, kg_ref,
                     wq_ref, wk_ref, wv_ref, q_out, k_out, v_out, qn_out, kn_out, first=True):
    t = qc_ref.shape[0]
    ang = pos_ref[...].astype(F32) * freq_ref[...]
    cos, sin = jnp.cos(ang), jnp.sin(ang)

    qct = qc_ref[...].astype(F32).T
    qcn = qct * lax.rsqrt(jnp.mean(qct * qct, axis=0, keepdims=True) + EPS) * gq_ref[...]
    q = jnp.dot(wq_ref[...], qcn.astype(BF16), preferred_element_type=F32)
    q = q.reshape(HEADS, QK_HEAD, t)
    q = q * lax.rsqrt(jnp.mean(q * q, axis=1, keepdims=True) + EPS) * qg_ref[...][None]
    r1, r2 = _rope_rows(q[:, QK_NOPE:, :], cos, sin)
    scale = (QK_HEAD ** -0.5) * LOG2E
    q_bf = (jnp.concatenate([q[:, :QK_NOPE, :], r1, r2], axis=1) * scale).astype(BF16)
    q_out[...] = jnp.concatenate([q_bf, jnp.zeros((HEADS, QK_PAD - QK_HEAD, t), BF16)], axis=1)
    q_f = q_bf.astype(F32)
    qn_out[...] = jnp.sqrt(jnp.sum(q_f * q_f, axis=1, keepdims=True))

    kvt = kv_ref[...].astype(F32).T
    kvn = (kvt * lax.rsqrt(jnp.mean(kvt * kvt, axis=0, keepdims=True) + EPS) * gkv_ref[...]).astype(BF16)
    kn = jnp.dot(wk_ref[...], kvn, preferred_element_type=F32).reshape(HEADS, QK_NOPE, t)
    v = jnp.dot(wv_ref[...], kvn, preferred_element_type=F32).reshape(HEADS, V_HEAD, t)
    pe = pe_ref[...].T[:QK_ROPE, :]
    ssq = jnp.sum(kn * kn, axis=1, keepdims=True) + jnp.sum(pe * pe, axis=0, keepdims=True)[None]
    rstd = lax.rsqrt(ssq * (1.0 / QK_HEAD) + EPS)
    kg = kg_ref[...]
    k_nope = kn * rstd * kg[None, :QK_NOPE, :]
    k_pe = (pe * kg[QK_NOPE:, :])[None] * rstd
    p1, p2 = _rope_rows(k_pe, cos, sin)
    k_real = jnp.concatenate([k_nope, p1, p2], axis=1)
    pad = jnp.where(lax.broadcasted_iota(jnp.int32, (HEADS, QK_PAD - QK_HEAD, t), 1) == 0, 1.0, 0.0)
    kt = jnp.concatenate([k_real, pad], axis=1)
    for h in range(HEADS):
        k_out[h] = kt[h].T.astype(BF16)
    k_f = k_real.astype(BF16).astype(F32)
    k_max = jnp.max(jnp.sum(k_f * k_f, axis=1, keepdims=True), axis=2, keepdims=True)
    k_norm = jnp.broadcast_to(jnp.sqrt(k_max), (HEADS, 1, LANES))
    kn_out[...] = k_norm if first else jnp.maximum(kn_out[...], k_norm)
    v_out[...] = jnp.concatenate([v, jnp.ones((HEADS, V_AUG - V_HEAD, t), F32)], axis=1).astype(BF16)


def _token_mix_prep_kernel(x0_ref, x1_ref, x2_ref, g_ref, w_ref, cw_ref, cb_ref, sh_ref, wa_ref, ba_ref, wx_ref,
                           bx_ref, lam_ref, pos_ref, freq_ref, gq_ref, gkv_ref, qg_ref, kg_ref, wq_ref, wk_ref, wv_ref,
                           ya_ref, gb_ref, q_out, k_out, v_out, qn_out, kn_out,
                           xr_s, gr_s, qc_s, kv_s, pe_s, gl_s, xprev, hprev):
    first = pl.program_id(0) == 0

    def project(x_ref, slot):
        _inproj_kernel(x_ref, g_ref, w_ref, xr_s.at[slot], gr_s.at[slot], qc_s.at[slot], kv_s.at[slot], pe_s.at[slot],
                       gl_s.at[slot])

    def branches(slot, a):
        gb_ref[pl.ds(a * TOK_TILE, TOK_TILE), :] = gl_s[slot, :, D_MODEL:]
        _lru_kernel(xr_s.at[slot], gr_s.at[slot], gl_s.at[slot, :, pl.ds(0, D_MODEL)], cw_ref, cb_ref, sh_ref, wa_ref,
                    ba_ref, wx_ref, bx_ref, lam_ref, ya_ref.at[pl.ds(a * TOK_TILE, TOK_TILE)], xprev, hprev)
        _mla_prep_kernel(qc_s.at[slot], kv_s.at[slot], pe_s.at[slot], pos_ref.at[a], freq_ref, gq_ref, gkv_ref,
                         qg_ref, kg_ref, wq_ref, wk_ref, wv_ref, q_out.at[:, a], k_out.at[:, a], v_out.at[:, a],
                         qn_out.at[:, a], kn_out.at[:, a])

    @pl.when(first)
    def _():
        xprev[...] = jnp.zeros((8, D_RNN), F32)
        hprev[...] = jnp.zeros((1, D_RNN), F32)
        project(x0_ref, 0)

    project(x1_ref, 1)
    branches(0, 0)
    project(x2_ref, 0)
    branches(1, 1)


def _token_mix_prep(x2, g, w_cat, conv_w, conv_b, wa, ba, wx, bx, lam, pos_row, freq, gq, gkv, qg, kg,
                    wq_t, wk_t, wv_t):
    s = x2.shape[0]
    t = TOK_TILE
    nt = s // t
    assert nt % 2 == 0
    x_tile = lambda off: pl.BlockSpec((t, D_MODEL), lambda i: (jnp.minimum(2 * i + off, nt - 1), 0))
    rows2 = lambda n: pl.BlockSpec((2 * t, n), lambda i: (i, 0))
    blk = lambda r, c: pl.BlockSpec((HEADS, 2, r, c), lambda i: (0, i, 0, 0))
    lru_w = _const_spec((LRU_BLOCKS, LRU_BLOCK_W, LRU_BLOCK_W))
    shifts = jnp.stack([jnp.eye(t, k=-(CONV_W - 1 - j), dtype=BF16) for j in range(CONV_W - 1)])
    slots = lambda n, dt: pltpu.VMEM((2, t, n), dt)
    return pl.pallas_call(
        _token_mix_prep_kernel,
        grid=(nt // 2,),
        in_specs=[x_tile(0), x_tile(1), x_tile(2), _const_spec((1, D_MODEL)),
                  pl.BlockSpec((D_MODEL, _N_IN), lambda i: (0, 0), pipeline_mode=pl.Buffered(1)),
                  _const_spec((CONV_W, D_RNN)), _const_spec((1, D_RNN)), _const_spec((CONV_W - 1, t, t)),
                  lru_w, _const_spec((1, D_RNN)), lru_w, _const_spec((1, D_RNN)), _const_spec((1, D_RNN)),
                  pl.BlockSpec((2, 1, t), lambda i: (i, 0, 0)),
                  _const_spec((QK_ROPE // 2, 1)), _const_spec((Q_LORA, 1)), _const_spec((KV_LORA, 1)),
                  _const_spec((QK_HEAD, 1)), _const_spec((QK_HEAD, 1)),
                  _const_spec((HEADS * QK_HEAD, Q_LORA)), _const_spec((HEADS * QK_NOPE, KV_LORA)),
                  _const_spec((HEADS * V_HEAD, KV_LORA))],
        out_specs=[rows2(D_RNN), rows2(D_MODEL),
                   blk(QK_PAD, t), blk(t, QK_PAD), blk(V_AUG, t), blk(1, t), blk(1, LANES)],
        out_shape=[jax.ShapeDtypeStruct((s, D_RNN), BF16), jax.ShapeDtypeStruct((s, D_MODEL), BF16),
                   jax.ShapeDtypeStruct((HEADS, nt, QK_PAD, t), BF16),
                   jax.ShapeDtypeStruct((HEADS, nt, t, QK_PAD), BF16),
                   jax.ShapeDtypeStruct((HEADS, nt, V_AUG, t), BF16),
                   jax.ShapeDtypeStruct((HEADS, nt, 1, t), F32),
                   jax.ShapeDtypeStruct((HEADS, nt, 1, LANES), F32)],
        scratch_shapes=[slots(D_RNN, BF16), slots(D_RNN, BF16), slots(Q_LORA, BF16), slots(KV_LORA, BF16),
                        slots(LANES, F32), slots(2 * D_MODEL, BF16),
                        pltpu.VMEM((8, D_RNN), F32), pltpu.VMEM((1, D_RNN), F32)],
        compiler_params=_cparams("arbitrary"),
        name="token_mix_prep",
    )(x2, x2, x2, g, w_cat, conv_w, conv_b, shifts, wa, ba, wx, bx, lam, pos_row, freq, gq, gkv, qg, kg,
      wq_t, wk_t, wv_t)


def _attn_kernel(fixed_ref, *refs):
    if fixed_ref:
        q_ref, k_ref, v_ref, qn_ref, kn_ref, o_ref, p_scr = refs
    else:
        q_ref, k_ref, v_ref, o_ref, s_scr = refs
    tq, tk = ATT_TQ, TOK_TILE
    nsub = tq // tk
    qi = pl.program_id(1)

    def query_block(g, a):
        q = q_ref[g, a]
        if not fixed_ref:
            return q
        k_max = jnp.max(kn_ref[g], axis=0)[:, :1]
        ref = qn_ref[g, a] * k_max
        row = lax.broadcasted_iota(jnp.int32, (QK_PAD - QK_HEAD, tk), 0)
        extra = jnp.where(row == 0, -ref, 0.0).astype(BF16)
        return jnp.concatenate([q[:QK_HEAD, :], extra], axis=0)

    qs = [jnp.concatenate([query_block(g, a) for a in range(nsub)], axis=1) for g in range(ATT_HEADS)]

    def finish(accs):
        for g in range(ATT_HEADS):
            o = accs[g][:V_HEAD, :] / accs[g][V_HEAD:V_HEAD + 1, :]
            for a in range(nsub):
                o_ref[g, a] = o[:, a * tk:(a + 1) * tk]

    if fixed_ref:
        chunk_gap = ((lax.broadcasted_iota(jnp.int32, (tk, tq), 0) >> CHUNK_SHIFT)
                     - (lax.broadcasted_iota(jnp.int32, (tk, tq), 1) >> CHUNK_SHIFT))

        def produce(kb, slot):
            lead = qi * (tq >> CHUNK_SHIFT) - kb * (tk >> CHUNK_SHIFT)
            for g in range(ATT_HEADS):
                s = jnp.dot(k_ref[g, kb], qs[g], preferred_element_type=F32)
                p_scr[slot, g] = jnp.exp2(jnp.where(chunk_gap <= lead, s, NEG_BIG)).astype(BF16)

        def accumulate(kb, slot, accs):
            return tuple(accs[g] + jnp.dot(v_ref[g, kb], p_scr[slot, g], preferred_element_type=F32)
                         for g in range(ATT_HEADS))

        def pair(i, accs):
            produce(2 * i + 1, 1)
            accs = accumulate(2 * i, 0, accs)
            produce(2 * i + 2, 0)
            return accumulate(2 * i + 1, 1, accs)

        assert nsub % 2 == 0
        n_pairs = (qi + 1) * (nsub // 2)
        accs = tuple(jnp.zeros((V_AUG, tq), F32) for _ in range(ATT_HEADS))
        produce(0, 0)
        accs = lax.fori_loop(0, n_pairs - 1, pair, accs)
        kb = 2 * (n_pairs - 1)
        produce(kb + 1, 1)
        finish(accumulate(kb + 1, 1, accumulate(kb, 0, accs)))
        return

    def prefetch(kb, slot):
        for g in range(ATT_HEADS):
            s_scr[slot, g] = jnp.dot(k_ref[g, kb], qs[g], preferred_element_type=F32)

    def consume(kb, slot, state, tail):
        out = []
        for g in range(ATT_HEADS):
            m, acc = state[g]
            if tail is None:
                visible = None
            else:
                kk = (lax.broadcasted_iota(jnp.int32, (tk, tq), 0) + tail * tk) >> CHUNK_SHIFT
                qq = lax.broadcasted_iota(jnp.int32, (tk, tq), 1) >> CHUNK_SHIFT
                visible = kk <= qq
            load = lambda: s_scr[slot, g] if visible is None else jnp.where(visible, s_scr[slot, g], NEG_BIG)
            m_new = jnp.maximum(m, jnp.max(load(), axis=0, keepdims=True))
            p = jnp.exp2(load() - m_new).astype(BF16)
            acc = acc * jnp.exp2(m - m_new) + jnp.dot(v_ref[g, kb], p, preferred_element_type=F32)
            out.append((m_new, acc))
        return tuple(out)

    def pair(i, state):
        for t in range(2):
            prefetch(2 * i + t + 1, (t + 1) % 2)
            state = consume(2 * i + t, t, state, None)
        return state

    assert nsub % 2 == 0
    state = tuple((jnp.full((1, tq), NEG_BIG, F32), jnp.zeros((V_AUG, tq), F32)) for _ in range(ATT_HEADS))
    prefetch(0, 0)
    state = lax.fori_loop(0, qi * (nsub // 2), pair, state)
    kb = qi * nsub
    for t in range(nsub):
        if t + 1 < nsub:
            prefetch(kb + t + 1, (t + 1) % 2)
        state = consume(kb + t, t % 2, state, t)
    finish(tuple(acc for _, acc in state))


def _attention(q_t, k, v_t, q_norm, k_norm, fixed_ref):
    heads, nt = q_t.shape[0], q_t.shape[1]
    s = nt * TOK_TILE
    assert s % ATT_TQ == 0 and heads % ATT_HEADS == 0
    qsub = ATT_TQ // TOK_TILE
    resident = dict(pipeline_mode=pl.Buffered(1))
    in_specs = [pl.BlockSpec((ATT_HEADS, qsub, QK_PAD, TOK_TILE), lambda h, i: (h, i, 0, 0)),
                pl.BlockSpec((ATT_HEADS, nt, TOK_TILE, QK_PAD), lambda h, i: (h, 0, 0, 0), **resident),
                pl.BlockSpec((ATT_HEADS, nt, V_AUG, TOK_TILE), lambda h, i: (h, 0, 0, 0), **resident)]
    args = (q_t, k, v_t)
    scratch = [pltpu.VMEM((2, ATT_HEADS, TOK_TILE, ATT_TQ), F32)]
    if fixed_ref:
        in_specs += [pl.BlockSpec((ATT_HEADS, qsub, 1, TOK_TILE), lambda h, i: (h, i, 0, 0)),
                     pl.BlockSpec((ATT_HEADS, nt, 1, LANES), lambda h, i: (h, 0, 0, 0))]
        args += (q_norm, k_norm)
        scratch = [pltpu.VMEM((2, ATT_HEADS, TOK_TILE, ATT_TQ), BF16)]
    return pl.pallas_call(
        functools.partial(_attn_kernel, fixed_ref),
        grid=(heads // ATT_HEADS, s // ATT_TQ),
        in_specs=in_specs,
        out_specs=pl.BlockSpec((ATT_HEADS, qsub, V_HEAD, TOK_TILE), lambda h, i: (h, i, 0, 0)),
        out_shape=jax.ShapeDtypeStruct((heads, nt, V_HEAD, TOK_TILE), F32),
        scratch_shapes=scratch,
        compiler_params=_cparams("arbitrary", "arbitrary"),
        name="attention_fixed_ref" if fixed_ref else "attention_online",
    )(*args)


def _first_lane(cond, lane):
    return jnp.min(jnp.where(cond, lane.astype(F32), 4.0 * LANES), axis=1, keepdims=True).astype(jnp.int32)


def _outproj_router_kernel(x_ref, ya_ref, gb_ref, yb_ref, wo_ref, gf_ref, wrh_ref, wrl_ref, br_ref,
                           h_ref, xn_ref, ri_ref, rf_ref, cnt_ref, base):
    t = TOK_TILE

    @pl.when(pl.program_id(0) == 0)
    def _():
        base[...] = jnp.zeros((1, LANES), F32)

    yb = yb_ref[...].reshape(HEADS * V_HEAD, t).T
    merged = ya_ref[...].astype(F32) + _sigmoid(gb_ref[...].astype(F32)) * yb
    h = x_ref[...] + jnp.dot(merged.astype(BF16), wo_ref[...], preferred_element_type=F32)
    h_ref[...] = h
    xn = h * lax.rsqrt(jnp.mean(h * h, axis=-1, keepdims=True) + EPS) * gf_ref[...]
    _to_row_tiles(xn_ref, xn)

    x_hi = xn.astype(BF16)
    x_lo = (xn - x_hi.astype(F32)).astype(BF16)
    logits = (jnp.dot(x_hi, wrh_ref[...], preferred_element_type=F32)
              + jnp.dot(x_hi, wrl_ref[...], preferred_element_type=F32)
              + jnp.dot(x_lo, wrh_ref[...], preferred_element_type=F32)) + br_ref[...]

    lane = lax.broadcasted_iota(jnp.int32, (t, LANES), 1)
    is_grp = jnp.logical_and(lane >= N_EXPERTS, lane < N_EXPERTS + N_GROUPS)
    gl = jnp.where(is_grp, logits, NEG_BIG)
    gmax = jnp.max(gl, axis=1, keepdims=True)
    grp = _first_lane(gl == gmax, lane) - N_EXPERTS
    p_grp = 1.0 / jnp.sum(jnp.where(is_grp, jnp.exp(gl - gmax), 0.0), axis=1, keepdims=True)
    el = jnp.where((lane >> GROUP_SHIFT) == grp, logits, NEG_BIG)
    v1 = jnp.max(el, axis=1, keepdims=True)
    e1 = _first_lane(el == v1, lane)
    el2 = jnp.where(lane == e1, NEG_BIG, el)
    v2 = jnp.max(el2, axis=1, keepdims=True)
    e2 = _first_lane(el2 == v2, lane)
    ex = jnp.exp(v2 - v1)
    w1 = p_grp / (1.0 + ex)
    w2 = p_grp * ex / (1.0 + ex)

    tri = (lax.broadcasted_iota(jnp.int32, (t, t), 1) < lax.broadcasted_iota(jnp.int32, (t, t), 0))
    tri = jnp.where(tri, 1.0, 0.0).astype(BF16)
    oh1 = jnp.where(lane == e1, 1.0, 0.0)
    oh2 = jnp.where(lane == e2, 1.0, 0.0)
    pre1 = jnp.dot(tri, oh1.astype(BF16), preferred_element_type=F32) + base[...]
    base1 = base[...] + jnp.sum(oh1, axis=0, keepdims=True)
    pre2 = jnp.dot(tri, oh2.astype(BF16), preferred_element_type=F32) + base1
    base2 = base1 + jnp.sum(oh2, axis=0, keepdims=True)
    r1 = jnp.sum(oh1 * pre1, axis=1, keepdims=True).astype(jnp.int32)
    r2 = jnp.sum(oh2 * pre2, axis=1, keepdims=True).astype(jnp.int32)
    base[...] = base2
    cnt_ref[...] = base2

    ri = jnp.where(lane == 0, e1, jnp.where(lane == 1, e2, jnp.where(lane == 2, r1, r2)))
    ri_ref[...] = ri.astype(F32).T[:8, :].astype(jnp.int32)
    rf_ref[...] = jnp.where(lane == 0, w1, w2)


def _outproj_router(x2, ya_g, gb, yb_t, w_out, gf, wr_hi, wr_lo, br):
    s = x2.shape[0]
    t = TOK_TILE
    row = lambda n: pl.BlockSpec((t, n), lambda i: (i, 0))
    return pl.pallas_call(
        _outproj_router_kernel,
        grid=(s // t,),
        in_specs=[row(D_MODEL), row(D_MODEL), row(D_MODEL),
                  pl.BlockSpec((HEADS, None, V_HEAD, t), lambda i: (0, i, 0, 0)),
                  _const_spec((D_MODEL, D_MODEL)), _const_spec((1, D_MODEL)),
                  _const_spec((D_MODEL, LANES)), _const_spec((D_MODEL, LANES)), _const_spec((1, LANES))],
        out_specs=[row(D_MODEL), pl.BlockSpec((t * ROW_TILE, LANES), lambda i: (i, 0)),
                   pl.BlockSpec((None, 8, t), lambda i: (i, 0, 0)), row(LANES),
                   _const_spec((1, LANES))],
        out_shape=[jax.ShapeDtypeStruct((s, D_MODEL), F32), jax.ShapeDtypeStruct((s * ROW_TILE, LANES), F32),
                   jax.ShapeDtypeStruct((s // t, 8, t), jnp.int32), jax.ShapeDtypeStruct((s, LANES), F32),
                   jax.ShapeDtypeStruct((1, LANES), F32)],
        scratch_shapes=[pltpu.VMEM((1, LANES), F32)],
        compiler_params=_cparams("arbitrary"),
        name="outproj_router",
    )(x2, ya_g, gb, yb_t, w_out, gf, wr_hi, wr_lo, br)


def _row_copy(src_ref, src_row, dst_ref, dst_row, sem):
    src = src_ref.at[pl.ds(pl.multiple_of(src_row * ROW_TILE, ROW_TILE), ROW_TILE), :]
    dst = dst_ref.at[pl.ds(pl.multiple_of(dst_row * ROW_TILE, ROW_TILE), ROW_TILE), :]
    return pltpu.make_async_copy(src, dst, sem)


def _slot(ps_ref, ri_ref, k, i):
    return ps_ref[ri_ref[k, i]] + ri_ref[2 + k, i]


def _scatter_kernel(ps_ref, ri_ref, xn_ref, zeros_ref, xpad_ref, sem):
    del zeros_ref
    t = TOK_TILE

    def issue(i, c):
        for k in range(2):
            _row_copy(xn_ref, i, xpad_ref, _slot(ps_ref, ri_ref, k, i), sem).start(priority=k)
        return c

    lax.fori_loop(0, t, issue, 0, unroll=DMA_UNROLL)

    def drain(i, c):
        for _ in range(2):
            _row_copy(xn_ref, 0, xpad_ref, 0, sem).wait()
        return c

    lax.fori_loop(0, t, drain, 0, unroll=DMA_UNROLL)


def _scatter_rows(p_start, ri_t, xn, xpad_zeros):
    s = xn.shape[0] // ROW_TILE
    t = TOK_TILE
    grid_spec = pltpu.PrefetchScalarGridSpec(
        num_scalar_prefetch=1,
        grid=(s // t,),
        in_specs=[pl.BlockSpec((None, 8, t), lambda i, ps: (i, 0, 0), memory_space=pltpu.SMEM),
                  pl.BlockSpec((t * ROW_TILE, LANES), lambda i, ps: (i, 0)), pl.BlockSpec(memory_space=pl.ANY)],
        out_specs=pl.BlockSpec(memory_space=pl.ANY),
        scratch_shapes=[pltpu.SemaphoreType.DMA(())],
    )
    return pl.pallas_call(
        _scatter_kernel,
        grid_spec=grid_spec,
        out_shape=jax.ShapeDtypeStruct(xpad_zeros.shape, F32),
        input_output_aliases={3: 0},
        compiler_params=_cparams("arbitrary"),
        name="moe_scatter",
    )(p_start, ri_t, xn, xpad_zeros)


def _expert_kernel(be_ref, na_ref, x_ref, wg_ref, wu_ref, wd_ref, y_ref, wgu_bf, wd_bf):
    b = pl.program_id(0)
    active = b < na_ref[0]

    @pl.when(jnp.logical_and(active, jnp.logical_or(b == 0, be_ref[b] != be_ref[jnp.maximum(b - 1, 0)])))
    def _():
        wgu_bf[:, :D_EXPERT] = wg_ref[...].astype(BF16)
        wgu_bf[:, D_EXPERT:] = wu_ref[...].astype(BF16)
        wd_bf[...] = wd_ref[...].astype(BF16)

    @pl.when(active)
    def _():
        x = _from_row_tiles(x_ref, MOE_BLOCK).astype(BF16)
        gu = jnp.dot(x, wgu_bf[...], preferred_element_type=F32)
        g, u = gu[:, :D_EXPERT], gu[:, D_EXPERT:]
        hid = (g * _sigmoid(g) * u).astype(BF16)
        _to_row_tiles(y_ref, jnp.dot(hid, wd_bf[...], preferred_element_type=F32))

    @pl.when(jnp.logical_not(active))
    def _():
        y_ref[...] = jnp.zeros_like(y_ref)


def _experts(blk_expert, n_active, xpad, w_gate, w_up, w_down):
    blk_rows = MOE_BLOCK * ROW_TILE
    nblk = xpad.shape[0] // blk_rows
    rows = lambda b, be, na: (jnp.minimum(b, na[0] - 1), 0)
    expert = lambda b, be, na: (be[b], 0, 0)
    grid_spec = pltpu.PrefetchScalarGridSpec(
        num_scalar_prefetch=2,
        grid=(nblk,),
        in_specs=[pl.BlockSpec((blk_rows, LANES), rows),
                  pl.BlockSpec((None, D_MODEL, D_EXPERT), expert), pl.BlockSpec((None, D_MODEL, D_EXPERT), expert),
                  pl.BlockSpec((None, D_EXPERT, D_MODEL), expert)],
        out_specs=pl.BlockSpec((blk_rows, LANES), lambda b, be, na: (b, 0)),
        scratch_shapes=[pltpu.VMEM((D_MODEL, 2 * D_EXPERT), BF16), pltpu.VMEM((D_EXPERT, D_MODEL), BF16)],
    )
    return pl.pallas_call(
        _expert_kernel,
        grid_spec=grid_spec,
        out_shape=jax.ShapeDtypeStruct(xpad.shape, F32),
        compiler_params=_cparams("arbitrary"),
        name="moe_experts",
    )(blk_expert, n_active, xpad, w_gate, w_up, w_down)


def _combine_kernel(ps_ref, ri_ref, rin_ref, h_ref, rf_ref, ypad_ref, out_ref, ybuf, sem):
    t = TOK_TILE
    step = pl.program_id(0)
    cur = step % 2

    def fetch(idx_ref, slot):
        def issue(i, c):
            for k in range(2):
                _row_copy(ypad_ref, _slot(ps_ref, idx_ref, k, i), ybuf.at[slot, k], i,
                          sem.at[slot]).start(priority=k)
            return c

        lax.fori_loop(0, t, issue, 0, unroll=DMA_UNROLL)

    @pl.when(step == 0)
    def _():
        fetch(ri_ref, 0)

    @pl.when(step + 1 < pl.num_programs(0))
    def _():
        fetch(rin_ref, 1 - cur)

    def drain(i, c):
        for k in range(2):
            _row_copy(ypad_ref, 0, ybuf.at[cur, k], 0, sem.at[cur]).wait()
        return c

    lax.fori_loop(0, t, drain, 0, unroll=DMA_UNROLL)
    rf = rf_ref[...]
    out_ref[...] = (h_ref[...] + rf[:, 0:1] * _from_row_tiles(ybuf.at[cur, 0], t)
                    + rf[:, 1:2] * _from_row_tiles(ybuf.at[cur, 1], t))


def _combine(p_start, ri_t, h, rf, ypad):
    s = h.shape[0]
    t = TOK_TILE
    nt = s // t
    idx = lambda shift: pl.BlockSpec((None, 8, t), lambda i, ps: (jnp.minimum(i + shift, nt - 1), 0, 0),
                                     memory_space=pltpu.SMEM)
    grid_spec = pltpu.PrefetchScalarGridSpec(
        num_scalar_prefetch=1,
        grid=(nt,),
        in_specs=[idx(0), idx(1), pl.BlockSpec((t, D_MODEL), lambda i, ps: (i, 0)),
                  pl.BlockSpec((t, LANES), lambda i, ps: (i, 0)), pl.BlockSpec(memory_space=pl.ANY)],
        out_specs=pl.BlockSpec((t, D_MODEL), lambda i, ps: (i, 0)),
        scratch_shapes=[pltpu.VMEM((2, 2, t * ROW_TILE, LANES), F32), pltpu.SemaphoreType.DMA((2,))],
    )
    return pl.pallas_call(
        _combine_kernel,
        grid_spec=grid_spec,
        out_shape=jax.ShapeDtypeStruct((s, D_MODEL), F32),
        compiler_params=_cparams("arbitrary"),
        name="moe_combine",
    )(p_start, ri_t, ri_t, h, rf, ypad)


def _layer(h3, positions, norm_mix_g, w_in, conv_w, conv_b, lru_wa, lru_ba, lru_wx, lru_bx, lru_lambda,
           q_a_g, w_uq, kv_a_g, w_ukv, q_norm_g, k_norm_g, w_out, norm_ffn_g,
           router_group_w, router_group_b, router_expert_w, router_expert_b, w_gate, w_up, w_down):
    b, s, _ = h3.shape
    assert b == 1 and s % TOK_TILE == 0
    x2 = h3.reshape(s, D_MODEL)
    nt = s // TOK_TILE

    o = np.cumsum((0, D_RNN, D_RNN, Q_LORA, KV_LORA, QK_ROPE))
    w_cat = jnp.concatenate(
        [w_in[:, o[0]:o[4]], w_in[:, o[4]:o[5]], jnp.zeros((D_MODEL, LANES - QK_ROPE), F32), w_in[:, o[5]:]],
        axis=1).astype(BF16)
    row = lambda v: v.reshape(1, -1)
    col = lambda v: v.reshape(-1, 1)
    wq_t = w_uq.T.astype(BF16)
    ukv = w_ukv.reshape(KV_LORA, HEADS, QK_NOPE + V_HEAD)
    wk_t = ukv[:, :, :QK_NOPE].reshape(KV_LORA, HEADS * QK_NOPE).T.astype(BF16)
    wv_t = ukv[:, :, QK_NOPE:].reshape(KV_LORA, HEADS * V_HEAD).T.astype(BF16)
    half = QK_ROPE // 2
    freq = col(ROPE_THETA ** (-jnp.arange(half, dtype=F32) / half))
    wr = jnp.concatenate([router_expert_w, router_group_w,
                          jnp.zeros((D_MODEL, LANES - N_EXPERTS - N_GROUPS), F32)], axis=1)
    wr_hi = wr.astype(BF16)
    wr_lo = (wr - wr_hi.astype(F32)).astype(BF16)
    br = row(jnp.concatenate([router_expert_b, router_group_b, jnp.zeros((LANES - N_EXPERTS - N_GROUPS,), F32)]))

    ya_g, gb, q_t, k_r, v_t, q_norm, k_norm = _token_mix_prep(
        x2, row(norm_mix_g), w_cat, conv_w, row(conv_b), lru_wa.astype(BF16), row(lru_ba), lru_wx.astype(BF16),
        row(lru_bx), row(lru_lambda), positions.reshape(nt, 1, TOK_TILE), freq, col(q_a_g), col(kv_a_g),
        col(q_norm_g), col(k_norm_g), wq_t, wk_t, wv_t)
    bound = jnp.max(jnp.max(q_norm, axis=(1, 2, 3)) * jnp.max(k_norm, axis=(1, 2, 3)))
    yb_t = lax.cond(bound <= ATT_FIXED_REF_LIMIT,
                    lambda: _attention(q_t, k_r, v_t, q_norm, k_norm, True),
                    lambda: _attention(q_t, k_r, v_t, q_norm, k_norm, False))
    h, xn, ri, rf, counts = _outproj_router(x2, ya_g, gb, yb_t, w_out.astype(BF16), row(norm_ffn_g),
                                            wr_hi, wr_lo, br)

    cnt = counts[0, :N_EXPERTS].astype(jnp.int32)
    padded = (cnt + MOE_BLOCK - 1) // MOE_BLOCK * MOE_BLOCK
    p_end = jnp.cumsum(padded)
    p_start = p_end - padded
    n_slots = 2 * s + N_EXPERTS * MOE_BLOCK
    nblk = n_slots // MOE_BLOCK
    blk_first = jnp.arange(nblk, dtype=jnp.int32) * MOE_BLOCK
    blk_expert = jnp.minimum(jnp.sum((p_end[None, :] <= blk_first[:, None]).astype(jnp.int32), axis=1),
                             N_EXPERTS - 1)
    n_active = (p_end[-1:] // MOE_BLOCK).astype(jnp.int32)
    p_start = p_start.astype(jnp.int32)

    xpad = _scatter_rows(p_start, ri, xn, jnp.zeros((n_slots * ROW_TILE, LANES), F32))
    ypad = _experts(blk_expert, n_active, xpad, w_gate, w_up, w_down)
    out = _combine(p_start, ri, h, rf, ypad)
    return out.reshape(b, s, D_MODEL)


def kernel(x, positions, norm_mix_g, w_in, conv_w, conv_b, lru_wa, lru_ba, lru_wx, lru_bx, lru_lambda, q_a_g,
           w_uq, kv_a_g, w_ukv, q_norm_g, k_norm_g, w_out, norm_ffn_g, router_group_w, router_group_b,
           router_expert_w, router_expert_b, w_gate, w_up, w_down):
    h = x
    for l in range(norm_mix_g.shape[0]):
        h = _layer(h, positions, norm_mix_g[l], w_in[l], conv_w[l], conv_b[l], lru_wa[l], lru_ba[l], lru_wx[l],
                   lru_bx[l], lru_lambda[l], q_a_g[l], w_uq[l], kv_a_g[l], w_ukv[l], q_norm_g[l], k_norm_g[l],
                   w_out[l], norm_ffn_g[l], router_group_w[l], router_group_b[l], router_expert_w[l],
                   router_expert_b[l], w_gate[l], w_up[l], w_down[l])
    return h
```

```python
import functools

import jax
import jax.numpy as jnp
import numpy as np
from jax import lax
from jax.experimental import pallas as pl
from jax.experimental.pallas import tpu as pltpu

F32 = jnp.float32
BF16 = jnp.bfloat16

D_MODEL = 1024
EPS = 1e-6
CHUNK = 64
D_RNN = 1024
LRU_BLOCKS = 4
LRU_BLOCK_W = 256
CONV_W = 4
LRU_C = 8.0
HEADS = 16
V_HEAD = 64
QK_NOPE = 64
QK_ROPE = 32
QK_HEAD = 96
Q_LORA = 768
KV_LORA = 256
ROPE_THETA = 10000.0
N_GROUPS = 4
EXPERTS_PER_GROUP = 8
N_EXPERTS = 32
D_EXPERT = 256
CHUNK_SHIFT = CHUNK.bit_length() - 1
GROUP_SHIFT = EXPERTS_PER_GROUP.bit_length() - 1

LANES = 128
TOK_TILE = 256
ATT_TQ = 512
ATT_HEADS = 4
ATT_FIXED_REF_LIMIT = 48.0
QK_PAD = 128
V_AUG = 80
MOE_BLOCK = 256
DMA_UNROLL = 8
NEG_BIG = -1e30
LOG2E = 1.4426950408889634
VMEM_LIMIT = 56 * 1024 * 1024

_SEG_XR, _SEG_GR, _SEG_QC, _SEG_KV, _SEG_PE, _SEG_GL = 0, 1024, 2048, 2816, 3072, 3200
_N_IN = 5248


def _cparams(*sem):
    return pltpu.CompilerParams(dimension_semantics=sem, vmem_limit_bytes=VMEM_LIMIT)


def _const_spec(shape):
    nd = len(shape)
    return pl.BlockSpec(shape, lambda *_: (0,) * nd)


ROW_TILE = D_MODEL // LANES


def _to_row_tiles(dst_ref, val):
    n = val.shape[0]
    for c in range(ROW_TILE):
        dst_ref[pl.ds(c, n, stride=ROW_TILE), :] = val[:, c * LANES:(c + 1) * LANES]


def _from_row_tiles(src_ref, n):
    return jnp.concatenate([src_ref[pl.ds(c, n, stride=ROW_TILE), :] for c in range(ROW_TILE)], axis=1)


def _inproj_input(x_ref, g_ref):
    x = x_ref[...]
    ms = jnp.mean(x * x, axis=-1, keepdims=True)
    return (x * lax.rsqrt(ms + EPS) * g_ref[...]).astype(BF16)


def _inproj_steps(u, w_ref, xr_ref, gr_ref, qc_ref, kv_ref, pe_ref, gl_ref):
    def chunk(lo, hi):
        return jnp.dot(u, w_ref[:, lo:hi], preferred_element_type=F32)

    def s0():
        xr_ref[...] = chunk(_SEG_XR, _SEG_GR).astype(BF16)

    def s1():
        gr_ref[...] = chunk(_SEG_GR, _SEG_QC).astype(BF16)

    def s2():
        p = chunk(_SEG_QC, _SEG_GL)
        qc_ref[...] = p[:, :Q_LORA].astype(BF16)
        kv_ref[...] = p[:, Q_LORA:Q_LORA + KV_LORA].astype(BF16)
        pe_ref[...] = p[:, Q_LORA + KV_LORA:]

    def s3():
        gl_ref[:, :D_MODEL] = chunk(_SEG_GL, _SEG_GL + D_MODEL).astype(BF16)

    def s4():
        gl_ref[:, D_MODEL:] = chunk(_SEG_GL + D_MODEL, _N_IN).astype(BF16)

    return [s0, s1, s2, s3, s4]


def _sigmoid(x):
    return 1.0 / (1.0 + jnp.exp(-x))


def _lru_front(xr_ref, cw_ref, cb_ref, sh_ref, wa_ref, wx_ref, xprev):
    t = xr_ref.shape[0]

    x_bf = xr_ref[...]
    x = x_bf.astype(F32)
    xa = cb_ref[...] + cw_ref[CONV_W - 1:CONV_W, :] * x
    top = jnp.zeros((8, D_RNN), F32)
    row8 = lax.broadcasted_iota(jnp.int32, (8, D_RNN), 0)
    for j in range(CONV_W - 1):
        shift = CONV_W - 1 - j
        xa = xa + cw_ref[j:j + 1, :] * jnp.dot(sh_ref[j], x_bf, preferred_element_type=F32)
        top = top + cw_ref[j:j + 1, :] * jnp.where(row8 < shift, pltpu.roll(xprev[...], shift, axis=0), 0.0)
    xa = jnp.concatenate([xa[:8, :] + top, xa[8:, :]], axis=0)
    xprev[...] = x[t - 8:t, :]

    r_parts, i_parts = [], []
    for b in range(LRU_BLOCKS):
        xb = xa[:, b * LRU_BLOCK_W:(b + 1) * LRU_BLOCK_W].astype(BF16)
        r_parts.append(jnp.dot(xb, wa_ref[b], preferred_element_type=F32))
        i_parts.append(jnp.dot(xb, wx_ref[b], preferred_element_type=F32))
    return xa, jnp.concatenate(r_parts, axis=1), jnp.concatenate(i_parts, axis=1)


def _lru_back(front, gr_ref, ga_ref, ba_ref, bx_ref, lam_ref, out_ref, hprev, cols):
    xa, r_lin, i_lin = (v[:, cols] for v in front)
    t, width = xa.shape
    r = _sigmoid(r_lin + ba_ref[:, cols])
    gi = _sigmoid(i_lin + bx_ref[:, cols])
    z = -lam_ref[:, cols]
    softplus = jnp.maximum(z, 0.0) + jnp.log(1.0 + jnp.exp(-jnp.abs(z)))
    log_a = (-LRU_C) * r * softplus
    a = jnp.exp(log_a)
    b_in = jnp.sqrt(1.0 - a * a) * (gi * xa)

    row = lax.broadcasted_iota(jnp.int32, (t, width), 0)
    d = 1
    while d < 8:
        valid = row >= d
        a_sh = pltpu.roll(a, d, axis=0)
        b_sh = pltpu.roll(b_in, d, axis=0)
        b_in = jnp.where(valid, a * b_sh, 0.0) + b_in
        a = jnp.where(valid, a * a_sh, a)
        d *= 2
    while d < t:
        b_in = jnp.concatenate([b_in[:d], a[d:] * b_in[:t - d] + b_in[d:]], axis=0)
        a = jnp.concatenate([a[:d], a[d:] * a[:t - d]], axis=0)
        d *= 2
    h = a * hprev[:, cols] + b_in
    hprev[:, cols] = h[t - 1:t, :]

    ya = h * jax.nn.gelu(gr_ref[:, cols].astype(F32))
    out_ref[:, cols] = (_sigmoid(ga_ref[:, cols].astype(F32)) * ya).astype(BF16)


def _rope_rows(x, cos, sin):
    half = QK_ROPE // 2
    x1, x2 = x[:, :half, :], x[:, half:, :]
    return x1 * cos - x2 * sin, x1 * sin + x2 * cos


def _mla_prep_kernel(qc_ref, kv_ref, pe_ref, pos_ref, freq_ref, gq_ref, gkv_ref, qg_ref, kg_ref,
                     wq_ref, wk_ref, wv_ref, q_out, k_out, v_out, qn_out, kn_out, first=True):
    t = qc_ref.shape[0]
    ang = pos_ref[...].astype(F32) * freq_ref[...]
    cos, sin = jnp.cos(ang), jnp.sin(ang)

    qct = qc_ref[...].astype(F32).T
    qcn = qct * lax.rsqrt(jnp.mean(qct * qct, axis=0, keepdims=True) + EPS) * gq_ref[...]
    q = jnp.dot(wq_ref[...], qcn.astype(BF16), preferred_element_type=F32)
    q = q.reshape(HEADS, QK_HEAD, t)
    q = q * lax.rsqrt(jnp.mean(q * q, axis=1, keepdims=True) + EPS) * qg_ref[...][None]
    r1, r2 = _rope_rows(q[:, QK_NOPE:, :], cos, sin)
    scale = (QK_HEAD ** -0.5) * LOG2E
    q_bf = (jnp.concatenate([q[:, :QK_NOPE, :], r1, r2], axis=1) * scale).astype(BF16)
    q_out[...] = jnp.concatenate([q_bf, jnp.zeros((HEADS, QK_PAD - QK_HEAD, t), BF16)], axis=1)
    q_f = q_bf.astype(F32)
    qn_out[...] = jnp.sqrt(jnp.sum(q_f * q_f, axis=1, keepdims=True))

    kvt = kv_ref[...].astype(F32).T
    kvn = (kvt * lax.rsqrt(jnp.mean(kvt * kvt, axis=0, keepdims=True) + EPS) * gkv_ref[...]).astype(BF16)
    kn = jnp.dot(wk_ref[...], kvn, preferred_element_type=F32).reshape(HEADS, QK_NOPE, t)
    v = jnp.dot(wv_ref[...], kvn, preferred_element_type=F32).reshape(HEADS, V_HEAD, t)
    pe = pe_ref[...].T[:QK_ROPE, :]
    ssq = jnp.sum(kn * kn, axis=1, keepdims=True) + jnp.sum(pe * pe, axis=0, keepdims=True)[None]
    rstd = lax.rsqrt(ssq * (1.0 / QK_HEAD) + EPS)
    kg = kg_ref[...]
    k_nope = kn * rstd * kg[None, :QK_NOPE, :]
    k_pe = (pe * kg[QK_NOPE:, :])[None] * rstd
    p1, p2 = _rope_rows(k_pe, cos, sin)
    k_real = jnp.concatenate([k_nope, p1, p2], axis=1)
    pad = jnp.where(lax.broadcasted_iota(jnp.int32, (HEADS, QK_PAD - QK_HEAD, t), 1) == 0, 1.0, 0.0)
    kt = jnp.concatenate([k_real, pad], axis=1)
    for h in range(HEADS):
        k_out[h] = kt[h].T.astype(BF16)
    k_f = k_real.astype(BF16).astype(F32)
    k_max = jnp.max(jnp.sum(k_f * k_f, axis=1, keepdims=True), axis=2, keepdims=True)
    k_norm = jnp.broadcast_to(jnp.sqrt(k_max), (HEADS, 1, LANES))
    kn_out[...] = k_norm if first else jnp.maximum(kn_out[...], k_norm)
    v_out[...] = jnp.concatenate([v, jnp.ones((HEADS, V_AUG - V_HEAD, t), F32)], axis=1).astype(BF16)


def _token_mix_prep_kernel(x0_ref, x1_ref, x2_ref, g_ref, w_ref, cw_ref, cb_ref, sh_ref, wa_ref, ba_ref, wx_ref,
                           bx_ref, lam_ref, pos_ref, freq_ref, gq_ref, gkv_ref, qg_ref, kg_ref, wq_ref, wk_ref, wv_ref,
                           ya_ref, gb_ref, q_out, k_out, v_out, qn_out, kn_out,
                           *scratch):
    first = pl.program_id(0) == 0
    slot_refs, (xprev, hprev) = (scratch[0:6], scratch[6:12]), scratch[12:]

    def project_steps(x_ref, slot):
        return _inproj_steps(_inproj_input(x_ref, g_ref), w_ref, *slot_refs[slot])

    def lru_front(slot):
        return _lru_front(slot_refs[slot][0], cw_ref, cb_ref, sh_ref, wa_ref, wx_ref, xprev)

    def branches(slot, a, proj):
        _, gr_s, qc_s, kv_s, pe_s, gl_s = slot_refs[slot]
        front = lru_front(slot)
        gb_ref[pl.ds(a * TOK_TILE, TOK_TILE), :] = gl_s[:, D_MODEL:]
        width = D_RNN // (len(proj) - 1)
        for c, step in enumerate(proj[:-1]):
            step()
            _lru_back(front, gr_s, gl_s, ba_ref, bx_ref, lam_ref, ya_ref.at[pl.ds(a * TOK_TILE, TOK_TILE)], hprev,
                      slice(c * width, (c + 1) * width))
        proj[-1]()
        _mla_prep_kernel(qc_s, kv_s, pe_s, pos_ref.at[a], freq_ref, gq_ref, gkv_ref,
                         qg_ref, kg_ref, wq_ref, wk_ref, wv_ref, q_out.at[:, a], k_out.at[:, a], v_out.at[:, a],
                         qn_out.at[:, a], kn_out.at[:, a])

    @pl.when(first)
    def _():
        xprev[...] = jnp.zeros((8, D_RNN), F32)
        hprev[...] = jnp.zeros((1, D_RNN), F32)
        for step in project_steps(x0_ref, 0):
            step()

    branches(0, 0, project_steps(x1_ref, 1))
    branches(1, 1, project_steps(x2_ref, 0))


def _token_mix_prep(x2, g, w_cat, conv_w, conv_b, wa, ba, wx, bx, lam, pos_row, freq, gq, gkv, qg, kg,
                    wq_t, wk_t, wv_t):
    s = x2.shape[0]
    t = TOK_TILE
    nt = s // t
    assert nt % 2 == 0
    x_tile = lambda off: pl.BlockSpec((t, D_MODEL), lambda i: (jnp.minimum(2 * i + off, nt - 1), 0))
    rows2 = lambda n: pl.BlockSpec((2 * t, n), lambda i: (i, 0))
    blk = lambda r, c: pl.BlockSpec((HEADS, 2, r, c), lambda i: (0, i, 0, 0))
    lru_w = _const_spec((LRU_BLOCKS, LRU_BLOCK_W, LRU_BLOCK_W))
    shifts = jnp.stack([jnp.eye(t, k=-(CONV_W - 1 - j), dtype=BF16) for j in range(CONV_W - 1)])
    slot = [pltpu.VMEM((t, D_RNN), BF16), pltpu.VMEM((t, D_RNN), BF16), pltpu.VMEM((t, Q_LORA), BF16),
            pltpu.VMEM((t, KV_LORA), BF16), pltpu.VMEM((t, LANES), F32), pltpu.VMEM((t, 2 * D_MODEL), BF16)]
    return pl.pallas_call(
        _token_mix_prep_kernel,
        grid=(nt // 2,),
        in_specs=[x_tile(0), x_tile(1), x_tile(2), _const_spec((1, D_MODEL)),
                  pl.BlockSpec((D_MODEL, _N_IN), lambda i: (0, 0), pipeline_mode=pl.Buffered(1)),
                  _const_spec((CONV_W, D_RNN)), _const_spec((1, D_RNN)), _const_spec((CONV_W - 1, t, t)),
                  lru_w, _const_spec((1, D_RNN)), lru_w, _const_spec((1, D_RNN)), _const_spec((1, D_RNN)),
                  pl.BlockSpec((2, 1, t), lambda i: (i, 0, 0)),
                  _const_spec((QK_ROPE // 2, 1)), _const_spec((Q_LORA, 1)), _const_spec((KV_LORA, 1)),
                  _const_spec((QK_HEAD, 1)), _const_spec((QK_HEAD, 1)),
                  _const_spec((HEADS * QK_HEAD, Q_LORA)), _const_spec((HEADS * QK_NOPE, KV_LORA)),
                  _const_spec((HEADS * V_HEAD, KV_LORA))],
        out_specs=[rows2(D_RNN), rows2(D_MODEL),
                   blk(QK_PAD, t), blk(t, QK_PAD), blk(V_AUG, t), blk(1, t), blk(1, LANES)],
        out_shape=[jax.ShapeDtypeStruct((s, D_RNN), BF16), jax.ShapeDtypeStruct((s, D_MODEL), BF16),
                   jax.ShapeDtypeStruct((HEADS, nt, QK_PAD, t), BF16),
                   jax.ShapeDtypeStruct((HEADS, nt, t, QK_PAD), BF16),
                   jax.ShapeDtypeStruct((HEADS, nt, V_AUG, t), BF16),
                   jax.ShapeDtypeStruct((HEADS, nt, 1, t), F32),
                   jax.ShapeDtypeStruct((HEADS, nt, 1, LANES), F32)],
        scratch_shapes=slot + slot + [pltpu.VMEM((8, D_RNN), F32), pltpu.VMEM((1, D_RNN), F32)],
        compiler_params=_cparams("arbitrary"),
        name="token_mix_prep",
    )(x2, x2, x2, g, w_cat, conv_w, conv_b, shifts, wa, ba, wx, bx, lam, pos_row, freq, gq, gkv, qg, kg,
      wq_t, wk_t, wv_t)


def _attn_kernel(fixed_ref, *refs):
    if fixed_ref:
        q_ref, k_ref, v_ref, qn_ref, kn_ref, o_ref, p_scr = refs
    else:
        q_ref, k_ref, v_ref, o_ref, s_scr = refs
    tq, tk = ATT_TQ, TOK_TILE
    nsub = tq // tk
    qi = pl.program_id(1)

    def query_block(g, a):
        q = q_ref[g, a]
        if not fixed_ref:
            return q
        k_max = jnp.max(kn_ref[g], axis=0)[:, :1]
        ref = qn_ref[g, a] * k_max
        row = lax.broadcasted_iota(jnp.int32, (QK_PAD - QK_HEAD, tk), 0)
        extra = jnp.where(row == 0, -ref, 0.0).astype(BF16)
        return jnp.concatenate([q[:QK_HEAD, :], extra], axis=0)

    qs = [jnp.concatenate([query_block(g, a) for a in range(nsub)], axis=1) for g in range(ATT_HEADS)]

    def finish(accs):
        for g in range(ATT_HEADS):
            o = accs[g][:V_HEAD, :] / accs[g][V_HEAD:V_HEAD + 1, :]
            for a in range(nsub):
                o_ref[g, a] = o[:, a * tk:(a + 1) * tk]

    if fixed_ref:
        chunk_gap = ((lax.broadcasted_iota(jnp.int32, (tk, tq), 0) >> CHUNK_SHIFT)
                     - (lax.broadcasted_iota(jnp.int32, (tk, tq), 1) >> CHUNK_SHIFT))

        def produce(kb, slot):
            lead = qi * (tq >> CHUNK_SHIFT) - kb * (tk >> CHUNK_SHIFT)
            for g in range(ATT_HEADS):
                s = jnp.dot(k_ref[g, kb], qs[g], preferred_element_type=F32)
                p_scr[slot, g] = jnp.exp2(jnp.where(chunk_gap <= lead, s, NEG_BIG)).astype(BF16)

        def accumulate(kb, slot, accs):
            return tuple(accs[g] + jnp.dot(v_ref[g, kb], p_scr[slot, g], preferred_element_type=F32)
                         for g in range(ATT_HEADS))

        def pair(i, accs):
            produce(2 * i + 1, 1)
            accs = accumulate(2 * i, 0, accs)
            produce(2 * i + 2, 0)
            return accumulate(2 * i + 1, 1, accs)

        assert nsub % 2 == 0
        n_pairs = (qi + 1) * (nsub // 2)
        accs = tuple(jnp.zeros((V_AUG, tq), F32) for _ in range(ATT_HEADS))
        produce(0, 0)
        accs = lax.fori_loop(0, n_pairs - 1, pair, accs)
        kb = 2 * (n_pairs - 1)
        produce(kb + 1, 1)
        finish(accumulate(kb + 1, 1, accumulate(kb, 0, accs)))
        return

    def prefetch(kb, slot):
        for g in range(ATT_HEADS):
            s_scr[slot, g] = jnp.dot(k_ref[g, kb], qs[g], preferred_element_type=F32)

    def consume(kb, slot, state, tail):
        out = []
        for g in range(ATT_HEADS):
            m, acc = state[g]
            if tail is None:
                visible = None
            else:
                kk = (lax.broadcasted_iota(jnp.int32, (tk, tq), 0) + tail * tk) >> CHUNK_SHIFT
                qq = lax.broadcasted_iota(jnp.int32, (tk, tq), 1) >> CHUNK_SHIFT
                visible = kk <= qq
            load = lambda: s_scr[slot, g] if visible is None else jnp.where(visible, s_scr[slot, g], NEG_BIG)
            m_new = jnp.maximum(m, jnp.max(load(), axis=0, keepdims=True))
            p = jnp.exp2(load() - m_new).astype(BF16)
            acc = acc * jnp.exp2(m - m_new) + jnp.dot(v_ref[g, kb], p, preferred_element_type=F32)
            out.append((m_new, acc))
        return tuple(out)

    def pair(i, state):
        for t in range(2):
            prefetch(2 * i + t + 1, (t + 1) % 2)
            state = consume(2 * i + t, t, state, None)
        return state

    assert nsub % 2 == 0
    state = tuple((jnp.full((1, tq), NEG_BIG, F32), jnp.zeros((V_AUG, tq), F32)) for _ in range(ATT_HEADS))
    prefetch(0, 0)
    state = lax.fori_loop(0, qi * (nsub // 2), pair, state)
    kb = qi * nsub
    for t in range(nsub):
        if t + 1 < nsub:
            prefetch(kb + t + 1, (t + 1) % 2)
        state = consume(kb + t, t % 2, state, t)
    finish(tuple(acc for _, acc in state))


def _attention(q_t, k, v_t, q_norm, k_norm, fixed_ref):
    heads, nt = q_t.shape[0], q_t.shape[1]
    s = nt * TOK_TILE
    assert s % ATT_TQ == 0 and heads % ATT_HEADS == 0
    qsub = ATT_TQ // TOK_TILE
    resident = dict(pipeline_mode=pl.Buffered(1))
    in_specs = [pl.BlockSpec((ATT_HEADS, qsub, QK_PAD, TOK_TILE), lambda h, i: (h, i, 0, 0)),
                pl.BlockSpec((ATT_HEADS, nt, TOK_TILE, QK_PAD), lambda h, i: (h, 0, 0, 0), **resident),
                pl.BlockSpec((ATT_HEADS, nt, V_AUG, TOK_TILE), lambda h, i: (h, 0, 0, 0), **resident)]
    args = (q_t, k, v_t)
    scratch = [pltpu.VMEM((2, ATT_HEADS, TOK_TILE, ATT_TQ), F32)]
    if fixed_ref:
        in_specs += [pl.BlockSpec((ATT_HEADS, qsub, 1, TOK_TILE), lambda h, i: (h, i, 0, 0)),
                     pl.BlockSpec((ATT_HEADS, nt, 1, LANES), lambda h, i: (h, 0, 0, 0))]
        args += (q_norm, k_norm)
        scratch = [pltpu.VMEM((2, ATT_HEADS, TOK_TILE, ATT_TQ), BF16)]
    return pl.pallas_call(
        functools.partial(_attn_kernel, fixed_ref),
        grid=(heads // ATT_HEADS, s // ATT_TQ),
        in_specs=in_specs,
        out_specs=pl.BlockSpec((ATT_HEADS, qsub, V_HEAD, TOK_TILE), lambda h, i: (h, i, 0, 0)),
        out_shape=jax.ShapeDtypeStruct((heads, nt, V_HEAD, TOK_TILE), F32),
        scratch_shapes=scratch,
        compiler_params=_cparams("arbitrary", "arbitrary"),
        name="attention_fixed_ref" if fixed_ref else "attention_online",
    )(*args)


def _first_lane(cond, lane):
    return jnp.min(jnp.where(cond, lane.astype(F32), 4.0 * LANES), axis=1, keepdims=True).astype(jnp.int32)


def _outproj_router_kernel(x_ref, ya_ref, gb_ref, yb_ref, wo_ref, gf_ref, wrh_ref, wrl_ref, br_ref,
                           h_ref, xn_ref, ri_ref, rf_ref, cnt_ref, base):
    t = TOK_TILE

    @pl.when(pl.program_id(0) == 0)
    def _():
        base[...] = jnp.zeros((1, LANES), F32)

    yb = yb_ref[...].reshape(HEADS * V_HEAD, t).T
    merged = ya_ref[...].astype(F32) + _sigmoid(gb_ref[...].astype(F32)) * yb
    h = x_ref[...] + jnp.dot(merged.astype(BF16), wo_ref[...], preferred_element_type=F32)
    h_ref[...] = h
    xn = h * lax.rsqrt(jnp.mean(h * h, axis=-1, keepdims=True) + EPS) * gf_ref[...]
    _to_row_tiles(xn_ref, xn)

    x_hi = xn.astype(BF16)
    x_lo = (xn - x_hi.astype(F32)).astype(BF16)
    logits = (jnp.dot(x_hi, wrh_ref[...], preferred_element_type=F32)
              + jnp.dot(x_hi, wrl_ref[...], preferred_element_type=F32)
              + jnp.dot(x_lo, wrh_ref[...], preferred_element_type=F32)) + br_ref[...]

    lane = lax.broadcasted_iota(jnp.int32, (t, LANES), 1)
    is_grp = jnp.logical_and(lane >= N_EXPERTS, lane < N_EXPERTS + N_GROUPS)
    gl = jnp.where(is_grp, logits, NEG_BIG)
    gmax = jnp.max(gl, axis=1, keepdims=True)
    grp = _first_lane(gl == gmax, lane) - N_EXPERTS
    p_grp = 1.0 / jnp.sum(jnp.where(is_grp, jnp.exp(gl - gmax), 0.0), axis=1, keepdims=True)
    el = jnp.where((lane >> GROUP_SHIFT) == grp, logits, NEG_BIG)
    v1 = jnp.max(el, axis=1, keepdims=True)
    e1 = _first_lane(el == v1, lane)
    el2 = jnp.where(lane == e1, NEG_BIG, el)
    v2 = jnp.max(el2, axis=1, keepdims=True)
    e2 = _first_lane(el2 == v2, lane)
    ex = jnp.exp(v2 - v1)
    w1 = p_grp / (1.0 + ex)
    w2 = p_grp * ex / (1.0 + ex)

    tri = (lax.broadcasted_iota(jnp.int32, (t, t), 1) < lax.broadcasted_iota(jnp.int32, (t, t), 0))
    tri = jnp.where(tri, 1.0, 0.0).astype(BF16)
    oh1 = jnp.where(lane == e1, 1.0, 0.0)
    oh2 = jnp.where(lane == e2, 1.0, 0.0)
    pre1 = jnp.dot(tri, oh1.astype(BF16), preferred_element_type=F32) + base[...]
    base1 = base[...] + jnp.sum(oh1, axis=0, keepdims=True)
    pre2 = jnp.dot(tri, oh2.astype(BF16), preferred_element_type=F32) + base1
    base2 = base1 + jnp.sum(oh2, axis=0, keepdims=True)
    r1 = jnp.sum(oh1 * pre1, axis=1, keepdims=True).astype(jnp.int32)
    r2 = jnp.sum(oh2 * pre2, axis=1, keepdims=True).astype(jnp.int32)
    base[...] = base2
    cnt_ref[...] = base2

    ri = jnp.where(lane == 0, e1, jnp.where(lane == 1, e2, jnp.where(lane == 2, r1, r2)))
    ri_ref[...] = ri.astype(F32).T[:8, :].astype(jnp.int32)
    rf_ref[...] = jnp.where(lane == 0, w1, w2)


def _outproj_router(x2, ya_g, gb, yb_t, w_out, gf, wr_hi, wr_lo, br):
    s = x2.shape[0]
    t = TOK_TILE
    row = lambda n: pl.BlockSpec((t, n), lambda i: (i, 0))
    return pl.pallas_call(
        _outproj_router_kernel,
        grid=(s // t,),
        in_specs=[row(D_MODEL), row(D_MODEL), row(D_MODEL),
                  pl.BlockSpec((HEADS, None, V_HEAD, t), lambda i: (0, i, 0, 0)),
                  _const_spec((D_MODEL, D_MODEL)), _const_spec((1, D_MODEL)),
                  _const_spec((D_MODEL, LANES)), _const_spec((D_MODEL, LANES)), _const_spec((1, LANES))],
        out_specs=[row(D_MODEL), pl.BlockSpec((t * ROW_TILE, LANES), lambda i: (i, 0)),
                   pl.BlockSpec((None, 8, t), lambda i: (i, 0, 0)), row(LANES),
                   _const_spec((1, LANES))],
        out_shape=[jax.ShapeDtypeStruct((s, D_MODEL), F32), jax.ShapeDtypeStruct((s * ROW_TILE, LANES), F32),
                   jax.ShapeDtypeStruct((s // t, 8, t), jnp.int32), jax.ShapeDtypeStruct((s, LANES), F32),
                   jax.ShapeDtypeStruct((1, LANES), F32)],
        scratch_shapes=[pltpu.VMEM((1, LANES), F32)],
        compiler_params=_cparams("arbitrary"),
        name="outproj_router",
    )(x2, ya_g, gb, yb_t, w_out, gf, wr_hi, wr_lo, br)


def _row_copy(src_ref, src_row, dst_ref, dst_row, sem):
    src = src_ref.at[pl.ds(pl.multiple_of(src_row * ROW_TILE, ROW_TILE), ROW_TILE), :]
    dst = dst_ref.at[pl.ds(pl.multiple_of(dst_row * ROW_TILE, ROW_TILE), ROW_TILE), :]
    return pltpu.make_async_copy(src, dst, sem)


def _scatter_kernel(sl_ref, xn_ref, zeros_ref, xpad_ref, sem):
    del zeros_ref
    t = TOK_TILE

    def issue(i, c):
        for k in range(2):
            _row_copy(xn_ref, i, xpad_ref, sl_ref[k, i], sem).start(priority=k)
        return c

    lax.fori_loop(0, t, issue, 0, unroll=DMA_UNROLL)

    def drain(i, c):
        for _ in range(2):
            _row_copy(xn_ref, 0, xpad_ref, 0, sem).wait()
        return c

    lax.fori_loop(0, t, drain, 0, unroll=DMA_UNROLL)


def _scatter_rows(slots, xn, xpad_zeros):
    s = xn.shape[0] // ROW_TILE
    t = TOK_TILE
    return pl.pallas_call(
        _scatter_kernel,
        grid=(s // t,),
        in_specs=[pl.BlockSpec((None, 8, t), lambda i: (i, 0, 0), memory_space=pltpu.SMEM),
                  pl.BlockSpec((t * ROW_TILE, LANES), lambda i: (i, 0)), pl.BlockSpec(memory_space=pl.ANY)],
        out_specs=pl.BlockSpec(memory_space=pl.ANY),
        out_shape=jax.ShapeDtypeStruct(xpad_zeros.shape, F32),
        scratch_shapes=[pltpu.SemaphoreType.DMA(())],
        input_output_aliases={2: 0},
        compiler_params=_cparams("arbitrary"),
        name="moe_scatter",
    )(slots, xn, xpad_zeros)


def _expert_kernel(be_ref, na_ref, x_ref, wg_ref, wu_ref, wd_ref, y_ref, wgu_bf, wd_bf):
    b = pl.program_id(0)
    active = b < na_ref[0]

    @pl.when(jnp.logical_and(active, jnp.logical_or(b == 0, be_ref[b] != be_ref[jnp.maximum(b - 1, 0)])))
    def _():
        wgu_bf[:, :D_EXPERT] = wg_ref[...].astype(BF16)
        wgu_bf[:, D_EXPERT:] = wu_ref[...].astype(BF16)
        wd_bf[...] = wd_ref[...].astype(BF16)

    @pl.when(active)
    def _():
        x = _from_row_tiles(x_ref, MOE_BLOCK).astype(BF16)
        gu = jnp.dot(x, wgu_bf[...], preferred_element_type=F32)
        g, u = gu[:, :D_EXPERT], gu[:, D_EXPERT:]
        hid = (g * _sigmoid(g) * u).astype(BF16)
        _to_row_tiles(y_ref, jnp.dot(hid, wd_bf[...], preferred_element_type=F32))

    @pl.when(jnp.logical_not(active))
    def _():
        y_ref[...] = jnp.zeros_like(y_ref)


def _experts(blk_expert, n_active, xpad, w_gate, w_up, w_down):
    blk_rows = MOE_BLOCK * ROW_TILE
    nblk = xpad.shape[0] // blk_rows
    rows = lambda b, be, na: (jnp.minimum(b, na[0] - 1), 0)
    expert = lambda b, be, na: (be[b], 0, 0)
    grid_spec = pltpu.PrefetchScalarGridSpec(
        num_scalar_prefetch=2,
        grid=(nblk,),
        in_specs=[pl.BlockSpec((blk_rows, LANES), rows),
                  pl.BlockSpec((None, D_MODEL, D_EXPERT), expert), pl.BlockSpec((None, D_MODEL, D_EXPERT), expert),
                  pl.BlockSpec((None, D_EXPERT, D_MODEL), expert)],
        out_specs=pl.BlockSpec((blk_rows, LANES), lambda b, be, na: (b, 0)),
        scratch_shapes=[pltpu.VMEM((D_MODEL, 2 * D_EXPERT), BF16), pltpu.VMEM((D_EXPERT, D_MODEL), BF16)],
    )
    return pl.pallas_call(
        _expert_kernel,
        grid_spec=grid_spec,
        out_shape=jax.ShapeDtypeStruct(xpad.shape, F32),
        compiler_params=_cparams("arbitrary"),
        name="moe_experts",
    )(blk_expert, n_active, xpad, w_gate, w_up, w_down)


def _combine_kernel(sl_ref, sln_ref, h_ref, rf_ref, ypad_ref, out_ref, ybuf, sem):
    t = TOK_TILE
    step = pl.program_id(0)
    cur = step % 2

    def fetch(idx_ref, slot):
        def issue(i, c):
            for k in range(2):
                _row_copy(ypad_ref, idx_ref[k, i], ybuf.at[slot, k], i,
                          sem.at[slot]).start(priority=k)
            return c

        lax.fori_loop(0, t, issue, 0, unroll=DMA_UNROLL)

    @pl.when(step == 0)
    def _():
        fetch(sl_ref, 0)

    @pl.when(step + 1 < pl.num_programs(0))
    def _():
        fetch(sln_ref, 1 - cur)

    def drain(i, c):
        for k in range(2):
            _row_copy(ypad_ref, 0, ybuf.at[cur, k], 0, sem.at[cur]).wait()
        return c

    lax.fori_loop(0, t, drain, 0, unroll=DMA_UNROLL)
    rf = rf_ref[...]
    out_ref[...] = (h_ref[...] + rf[:, 0:1] * _from_row_tiles(ybuf.at[cur, 0], t)
                    + rf[:, 1:2] * _from_row_tiles(ybuf.at[cur, 1], t))


def _combine(slots, h, rf, ypad):
    s = h.shape[0]
    t = TOK_TILE
    nt = s // t
    idx = lambda shift: pl.BlockSpec((None, 8, t), lambda i: (jnp.minimum(i + shift, nt - 1), 0, 0),
                                     memory_space=pltpu.SMEM)
    return pl.pallas_call(
        _combine_kernel,
        grid=(nt,),
        in_specs=[idx(0), idx(1), pl.BlockSpec((t, D_MODEL), lambda i: (i, 0)),
                  pl.BlockSpec((t, LANES), lambda i: (i, 0)), pl.BlockSpec(memory_space=pl.ANY)],
        out_specs=pl.BlockSpec((t, D_MODEL), lambda i: (i, 0)),
        out_shape=jax.ShapeDtypeStruct((s, D_MODEL), F32),
        scratch_shapes=[pltpu.VMEM((2, 2, t * ROW_TILE, LANES), F32), pltpu.SemaphoreType.DMA((2,))],
        compiler_params=_cparams("arbitrary"),
        name="moe_combine",
    )(slots, slots, h, rf, ypad)


def _layer(h3, positions, norm_mix_g, w_in, conv_w, conv_b, lru_wa, lru_ba, lru_wx, lru_bx, lru_lambda,
           q_a_g, w_uq, kv_a_g, w_ukv, q_norm_g, k_norm_g, w_out, norm_ffn_g,
           router_group_w, router_group_b, router_expert_w, router_expert_b, w_gate, w_up, w_down):
    b, s, _ = h3.shape
    assert b == 1 and s % TOK_TILE == 0
    x2 = h3.reshape(s, D_MODEL)
    nt = s // TOK_TILE

    o = np.cumsum((0, D_RNN, D_RNN, Q_LORA, KV_LORA, QK_ROPE))
    w_cat = jnp.concatenate(
        [w_in[:, o[0]:o[4]], w_in[:, o[4]:o[5]], jnp.zeros((D_MODEL, LANES - QK_ROPE), F32), w_in[:, o[5]:]],
        axis=1).astype(BF16)
    row = lambda v: v.reshape(1, -1)
    col = lambda v: v.reshape(-1, 1)
    wq_t = w_uq.T.astype(BF16)
    ukv = w_ukv.reshape(KV_LORA, HEADS, QK_NOPE + V_HEAD)
    wk_t = ukv[:, :, :QK_NOPE].reshape(KV_LORA, HEADS * QK_NOPE).T.astype(BF16)
    wv_t = ukv[:, :, QK_NOPE:].reshape(KV_LORA, HEADS * V_HEAD).T.astype(BF16)
    half = QK_ROPE // 2
    freq = col(ROPE_THETA ** (-jnp.arange(half, dtype=F32) / half))
    wr = jnp.concatenate([router_expert_w, router_group_w,
                          jnp.zeros((D_MODEL, LANES - N_EXPERTS - N_GROUPS), F32)], axis=1)
    wr_hi = wr.astype(BF16)
    wr_lo = (wr - wr_hi.astype(F32)).astype(BF16)
    br = row(jnp.concatenate([router_expert_b, router_group_b, jnp.zeros((LANES - N_EXPERTS - N_GROUPS,), F32)]))

    ya_g, gb, q_t, k_r, v_t, q_norm, k_norm = _token_mix_prep(
        x2, row(norm_mix_g), w_cat, conv_w, row(conv_b), lru_wa.astype(BF16), row(lru_ba), lru_wx.astype(BF16),
        row(lru_bx), row(lru_lambda), positions.reshape(nt, 1, TOK_TILE), freq, col(q_a_g), col(kv_a_g),
        col(q_norm_g), col(k_norm_g), wq_t, wk_t, wv_t)
    bound = jnp.max(jnp.max(q_norm, axis=(1, 2, 3)) * jnp.max(k_norm, axis=(1, 2, 3)))
    yb_t = lax.cond(bound <= ATT_FIXED_REF_LIMIT,
                    lambda: _attention(q_t, k_r, v_t, q_norm, k_norm, True),
                    lambda: _attention(q_t, k_r, v_t, q_norm, k_norm, False))
    h, xn, ri, rf, counts = _outproj_router(x2, ya_g, gb, yb_t, w_out.astype(BF16), row(norm_ffn_g),
                                            wr_hi, wr_lo, br)

    cnt = counts[0, :N_EXPERTS].astype(jnp.int32)
    padded = (cnt + MOE_BLOCK - 1) // MOE_BLOCK * MOE_BLOCK
    p_end = jnp.cumsum(padded)
    p_start = p_end - padded
    n_slots = 2 * s + N_EXPERTS * MOE_BLOCK
    nblk = n_slots // MOE_BLOCK
    blk_first = jnp.arange(nblk, dtype=jnp.int32) * MOE_BLOCK
    blk_expert = jnp.minimum(jnp.sum((p_end[None, :] <= blk_first[:, None]).astype(jnp.int32), axis=1),
                             N_EXPERTS - 1)
    n_active = (p_end[-1:] // MOE_BLOCK).astype(jnp.int32)
    hit = ri[:, 0:2, :, None] == jnp.arange(N_EXPERTS, dtype=jnp.int32)
    slots = ri[:, 2:4, :] + jnp.sum(jnp.where(hit, p_start.astype(jnp.int32), 0), axis=-1)
    slots = jnp.pad(slots, ((0, 0), (0, 6), (0, 0)))

    xpad = _scatter_rows(slots, xn, jnp.zeros((n_slots * ROW_TILE, LANES), F32))
    ypad = _experts(blk_expert, n_active, xpad, w_gate, w_up, w_down)
    out = _combine(slots, h, rf, ypad)
    return out.reshape(b, s, D_MODEL)


def kernel(x, positions, norm_mix_g, w_in, conv_w, conv_b, lru_wa, lru_ba, lru_wx, lru_bx, lru_lambda, q_a_g,
           w_uq, kv_a_g, w_ukv, q_norm_g, k_norm_g, w_out, norm_ffn_g, router_group_w, router_group_b,
           router_expert_w, router_expert_b, w_gate, w_up, w_down):
    h = x
    for l in range(norm_mix_g.shape[0]):
        h = _layer(h, positions, norm_mix_g[l], w_in[l], conv_w[l], conv_b[l], lru_wa[l], lru_ba[l], lru_wx[l],
                   lru_bx[l], lru_lambda[l], q_a_g[l], w_uq[l], kv_a_g[l], w_ukv[l], q_norm_g[l], k_norm_g[l],
                   w_out[l], norm_ffn_g[l], router_group_w[l], router_group_b[l], router_expert_w[l],
                   router_expert_b[l], w_gate[l], w_up[l], w_down[l])
    return h
```

```python
import functools

import jax
import jax.numpy as jnp
import numpy as np
from jax import lax
from jax.experimental import pallas as pl
from jax.experimental.pallas import tpu as pltpu

F32 = jnp.float32
BF16 = jnp.bfloat16

D_MODEL = 1024
EPS = 1e-6
CHUNK = 64
D_RNN = 1024
LRU_BLOCKS = 4
LRU_BLOCK_W = 256
CONV_W = 4
LRU_C = 8.0
HEADS = 16
V_HEAD = 64
QK_NOPE = 64
QK_ROPE = 32
QK_HEAD = 96
Q_LORA = 768
KV_LORA = 256
ROPE_THETA = 10000.0
N_GROUPS = 4
EXPERTS_PER_GROUP = 8
N_EXPERTS = 32
D_EXPERT = 256
CHUNK_SHIFT = CHUNK.bit_length() - 1
GROUP_SHIFT = EXPERTS_PER_GROUP.bit_length() - 1

LANES = 128
TOK_TILE = 256
ATT_TQ = 512
ATT_HEADS = 4
ATT_FIXED_REF_LIMIT = 48.0
QK_PAD = 128
V_AUG = 80
MOE_BLOCK = 256
DMA_UNROLL = 8
SCATTER_TILES = 4
NEG_BIG = -1e30
LOG2E = 1.4426950408889634
VMEM_LIMIT = 56 * 1024 * 1024

_SEG_XR, _SEG_GR, _SEG_QC, _SEG_KV, _SEG_PE, _SEG_GL = 0, 1024, 2048, 2816, 3072, 3200
_N_IN = 5248


def _cparams(*sem):
    return pltpu.CompilerParams(dimension_semantics=sem, vmem_limit_bytes=VMEM_LIMIT)


def _const_spec(shape):
    nd = len(shape)
    return pl.BlockSpec(shape, lambda *_: (0,) * nd)


ROW_TILE = D_MODEL // LANES


def _to_row_tiles(dst_ref, val):
    n = val.shape[0]
    for c in range(ROW_TILE):
        dst_ref[pl.ds(c, n, stride=ROW_TILE), :] = val[:, c * LANES:(c + 1) * LANES]


def _from_row_tiles(src_ref, n):
    return jnp.concatenate([src_ref[pl.ds(c, n, stride=ROW_TILE), :] for c in range(ROW_TILE)], axis=1)


def _inproj_input(x_ref, g_ref):
    x = x_ref[...]
    ms = jnp.mean(x * x, axis=-1, keepdims=True)
    return (x * lax.rsqrt(ms + EPS) * g_ref[...]).astype(BF16)


def _inproj_steps(u, w_ref, xr_ref, gr_ref, qc_ref, kv_ref, pe_ref, gl_ref):
    def chunk(lo, hi):
        return jnp.dot(u, w_ref[:, lo:hi], preferred_element_type=F32)

    def s0():
        xr_ref[...] = chunk(_SEG_XR, _SEG_GR).astype(BF16)

    def s1():
        gr_ref[...] = chunk(_SEG_GR, _SEG_QC).astype(BF16)

    def s2():
        p = chunk(_SEG_QC, _SEG_GL)
        qc_ref[...] = p[:, :Q_LORA].astype(BF16)
        kv_ref[...] = p[:, Q_LORA:Q_LORA + KV_LORA].astype(BF16)
        pe_ref[...] = p[:, Q_LORA + KV_LORA:]

    def s3():
        gl_ref[:, :D_MODEL] = chunk(_SEG_GL, _SEG_GL + D_MODEL).astype(BF16)

    def s4():
        gl_ref[:, D_MODEL:] = chunk(_SEG_GL + D_MODEL, _N_IN).astype(BF16)

    return [s0, s1, s2, s3, s4]


def _sigmoid(x):
    return 1.0 / (1.0 + jnp.exp(-x))


def _lru_front(xr_ref, cw_ref, cb_ref, sh_ref, wa_ref, wx_ref, xprev):
    t = xr_ref.shape[0]

    x_bf = xr_ref[...]
    x = x_bf.astype(F32)
    xa = cb_ref[...] + cw_ref[CONV_W - 1:CONV_W, :] * x
    top = jnp.zeros((8, D_RNN), F32)
    row8 = lax.broadcasted_iota(jnp.int32, (8, D_RNN), 0)
    for j in range(CONV_W - 1):
        shift = CONV_W - 1 - j
        xa = xa + cw_ref[j:j + 1, :] * jnp.dot(sh_ref[j], x_bf, preferred_element_type=F32)
        top = top + cw_ref[j:j + 1, :] * jnp.where(row8 < shift, pltpu.roll(xprev[...], shift, axis=0), 0.0)
    xa = jnp.concatenate([xa[:8, :] + top, xa[8:, :]], axis=0)
    xprev[...] = x[t - 8:t, :]

    r_parts, i_parts = [], []
    for b in range(LRU_BLOCKS):
        xb = xa[:, b * LRU_BLOCK_W:(b + 1) * LRU_BLOCK_W].astype(BF16)
        r_parts.append(jnp.dot(xb, wa_ref[b], preferred_element_type=F32))
        i_parts.append(jnp.dot(xb, wx_ref[b], preferred_element_type=F32))
    return xa, jnp.concatenate(r_parts, axis=1), jnp.concatenate(i_parts, axis=1)


def _lru_back(front, gr_ref, ga_ref, ba_ref, bx_ref, lam_ref, out_ref, hprev, cols):
    xa, r_lin, i_lin = (v[:, cols] for v in front)
    t, width = xa.shape
    r = _sigmoid(r_lin + ba_ref[:, cols])
    gi = _sigmoid(i_lin + bx_ref[:, cols])
    z = -lam_ref[:, cols]
    softplus = jnp.maximum(z, 0.0) + jnp.log(1.0 + jnp.exp(-jnp.abs(z)))
    log_a = (-LRU_C) * r * softplus
    a = jnp.exp(log_a)
    b_in = jnp.sqrt(1.0 - a * a) * (gi * xa)

    row = lax.broadcasted_iota(jnp.int32, (t, width), 0)
    d = 1
    while d < 8:
        valid = row >= d
        a_sh = pltpu.roll(a, d, axis=0)
        b_sh = pltpu.roll(b_in, d, axis=0)
        b_in = jnp.where(valid, a * b_sh, 0.0) + b_in
        a = jnp.where(valid, a * a_sh, a)
        d *= 2
    while d < t:
        b_in = jnp.concatenate([b_in[:d], a[d:] * b_in[:t - d] + b_in[d:]], axis=0)
        a = jnp.concatenate([a[:d], a[d:] * a[:t - d]], axis=0)
        d *= 2
    h = a * hprev[:, cols] + b_in
    hprev[:, cols] = h[t - 1:t, :]

    ya = h * jax.nn.gelu(gr_ref[:, cols].astype(F32))
    out_ref[:, cols] = (_sigmoid(ga_ref[:, cols].astype(F32)) * ya).astype(BF16)


def _rope_rows(x, cos, sin):
    half = QK_ROPE // 2
    x1, x2 = x[:, :half, :], x[:, half:, :]
    return x1 * cos - x2 * sin, x1 * sin + x2 * cos


def _mla_prep_kernel(qc_ref, kv_ref, pe_ref, pos_ref, freq_ref, gq_ref, gkv_ref, qg_ref, kg_ref,
                     wq_ref, wk_ref, wv_ref, q_out, k_out, v_out, qn_out, kn_out):
    t = qc_ref.shape[0]
    ang = pos_ref[...].astype(F32) * freq_ref[...]
    cos, sin = jnp.cos(ang), jnp.sin(ang)

    qct = qc_ref[...].astype(F32).T
    qcn = qct * lax.rsqrt(jnp.mean(qct * qct, axis=0, keepdims=True) + EPS) * gq_ref[...]
    q = jnp.dot(wq_ref[...], qcn.astype(BF16), preferred_element_type=F32)
    q = q.reshape(HEADS, QK_HEAD, t)
    q = q * lax.rsqrt(jnp.mean(q * q, axis=1, keepdims=True) + EPS) * qg_ref[...][None]
    r1, r2 = _rope_rows(q[:, QK_NOPE:, :], cos, sin)
    scale = (QK_HEAD ** -0.5) * LOG2E
    q_bf = (jnp.concatenate([q[:, :QK_NOPE, :], r1, r2], axis=1) * scale).astype(BF16)
    q_out[...] = jnp.concatenate([q_bf, jnp.zeros((HEADS, QK_PAD - QK_HEAD, t), BF16)], axis=1)
    q_f = q_bf.astype(F32)
    qn_out[...] = jnp.sqrt(jnp.sum(q_f * q_f, axis=1, keepdims=True))

    kvt = kv_ref[...].astype(F32).T
    kvn = (kvt * lax.rsqrt(jnp.mean(kvt * kvt, axis=0, keepdims=True) + EPS) * gkv_ref[...]).astype(BF16)
    kn = jnp.dot(wk_ref[...], kvn, preferred_element_type=F32).reshape(HEADS, QK_NOPE, t)
    v = jnp.dot(wv_ref[...], kvn, preferred_element_type=F32).reshape(HEADS, V_HEAD, t)
    pe = pe_ref[...].T[:QK_ROPE, :]
    ssq = jnp.sum(kn * kn, axis=1, keepdims=True) + jnp.sum(pe * pe, axis=0, keepdims=True)[None]
    rstd = lax.rsqrt(ssq * (1.0 / QK_HEAD) + EPS)
    kg = kg_ref[...]
    k_nope = kn * rstd * kg[None, :QK_NOPE, :]
    k_pe = (pe * kg[QK_NOPE:, :])[None] * rstd
    p1, p2 = _rope_rows(k_pe, cos, sin)
    k_real = jnp.concatenate([k_nope, p1, p2], axis=1)
    pad = jnp.where(lax.broadcasted_iota(jnp.int32, (HEADS, QK_PAD - QK_HEAD, t), 1) == 0, 1.0, 0.0)
    kt = jnp.concatenate([k_real, pad], axis=1)
    for h in range(HEADS):
        k_out[h] = kt[h].T.astype(BF16)
    k_f = k_real.astype(BF16).astype(F32)
    k_max = jnp.max(jnp.sum(k_f * k_f, axis=1, keepdims=True), axis=2, keepdims=True)
    kn_out[...] = jnp.broadcast_to(jnp.sqrt(k_max), (HEADS, 1, LANES))
    v_out[...] = jnp.concatenate([v, jnp.ones((HEADS, V_AUG - V_HEAD, t), F32)], axis=1).astype(BF16)


def _token_mix_prep_kernel(x0_ref, x1_ref, x2_ref, g_ref, w_ref, cw_ref, cb_ref, sh_ref, wa_ref, ba_ref, wx_ref,
                           bx_ref, lam_ref, pos_ref, freq_ref, gq_ref, gkv_ref, qg_ref, kg_ref, wq_ref, wk_ref, wv_ref,
                           ya_ref, gb_ref, q_out, k_out, v_out, qn_out, kn_out,
                           *scratch):
    first = pl.program_id(0) == 0
    slot_refs, (xprev, hprev) = (scratch[0:6], scratch[6:12]), scratch[12:]

    def project_steps(x_ref, slot):
        return _inproj_steps(_inproj_input(x_ref, g_ref), w_ref, *slot_refs[slot])

    def lru_front(slot):
        return _lru_front(slot_refs[slot][0], cw_ref, cb_ref, sh_ref, wa_ref, wx_ref, xprev)

    def branches(slot, a, proj):
        _, gr_s, qc_s, kv_s, pe_s, gl_s = slot_refs[slot]
        front = lru_front(slot)
        gb_ref[pl.ds(a * TOK_TILE, TOK_TILE), :] = gl_s[:, D_MODEL:]
        width = D_RNN // (len(proj) - 1)
        for c, step in enumerate(proj[:-1]):
            step()
            _lru_back(front, gr_s, gl_s, ba_ref, bx_ref, lam_ref, ya_ref.at[pl.ds(a * TOK_TILE, TOK_TILE)], hprev,
                      slice(c * width, (c + 1) * width))
        proj[-1]()
        _mla_prep_kernel(qc_s, kv_s, pe_s, pos_ref.at[a], freq_ref, gq_ref, gkv_ref,
                         qg_ref, kg_ref, wq_ref, wk_ref, wv_ref, q_out.at[:, a], k_out.at[:, a], v_out.at[:, a],
                         qn_out.at[:, a], kn_out.at[:, a])

    @pl.when(first)
    def _():
        xprev[...] = jnp.zeros((8, D_RNN), F32)
        hprev[...] = jnp.zeros((1, D_RNN), F32)
        for step in project_steps(x0_ref, 0):
            step()

    branches(0, 0, project_steps(x1_ref, 1))
    branches(1, 1, project_steps(x2_ref, 0))


def _token_mix_prep(x2, g, w_cat, conv_w, conv_b, wa, ba, wx, bx, lam, pos_row, freq, gq, gkv, qg, kg,
                    wq_t, wk_t, wv_t):
    s = x2.shape[0]
    t = TOK_TILE
    nt = s // t
    assert nt % 2 == 0
    x_tile = lambda off: pl.BlockSpec((t, D_MODEL), lambda i: (jnp.minimum(2 * i + off, nt - 1), 0))
    rows2 = lambda n: pl.BlockSpec((2 * t, n), lambda i: (i, 0))
    blk = lambda r, c: pl.BlockSpec((HEADS, 2, r, c), lambda i: (0, i, 0, 0))
    lru_w = _const_spec((LRU_BLOCKS, LRU_BLOCK_W, LRU_BLOCK_W))
    shifts = jnp.stack([jnp.eye(t, k=-(CONV_W - 1 - j), dtype=BF16) for j in range(CONV_W - 1)])
    slot = [pltpu.VMEM((t, D_RNN), BF16), pltpu.VMEM((t, D_RNN), BF16), pltpu.VMEM((t, Q_LORA), BF16),
            pltpu.VMEM((t, KV_LORA), BF16), pltpu.VMEM((t, LANES), F32), pltpu.VMEM((t, 2 * D_MODEL), BF16)]
    return pl.pallas_call(
        _token_mix_prep_kernel,
        grid=(nt // 2,),
        in_specs=[x_tile(0), x_tile(1), x_tile(2), _const_spec((1, D_MODEL)),
                  pl.BlockSpec((D_MODEL, _N_IN), lambda i: (0, 0), pipeline_mode=pl.Buffered(1)),
                  _const_spec((CONV_W, D_RNN)), _const_spec((1, D_RNN)), _const_spec((CONV_W - 1, t, t)),
                  lru_w, _const_spec((1, D_RNN)), lru_w, _const_spec((1, D_RNN)), _const_spec((1, D_RNN)),
                  pl.BlockSpec((2, 1, t), lambda i: (i, 0, 0)),
                  _const_spec((QK_ROPE // 2, 1)), _const_spec((Q_LORA, 1)), _const_spec((KV_LORA, 1)),
                  _const_spec((QK_HEAD, 1)), _const_spec((QK_HEAD, 1)),
                  _const_spec((HEADS * QK_HEAD, Q_LORA)), _const_spec((HEADS * QK_NOPE, KV_LORA)),
                  _const_spec((HEADS * V_HEAD, KV_LORA))],
        out_specs=[rows2(D_RNN), rows2(D_MODEL),
                   blk(QK_PAD, t), blk(t, QK_PAD), blk(V_AUG, t), blk(1, t), blk(1, LANES)],
        out_shape=[jax.ShapeDtypeStruct((s, D_RNN), BF16), jax.ShapeDtypeStruct((s, D_MODEL), BF16),
                   jax.ShapeDtypeStruct((HEADS, nt, QK_PAD, t), BF16),
                   jax.ShapeDtypeStruct((HEADS, nt, t, QK_PAD), BF16),
                   jax.ShapeDtypeStruct((HEADS, nt, V_AUG, t), BF16),
                   jax.ShapeDtypeStruct((HEADS, nt, 1, t), F32),
                   jax.ShapeDtypeStruct((HEADS, nt, 1, LANES), F32)],
        scratch_shapes=slot + slot + [pltpu.VMEM((8, D_RNN), F32), pltpu.VMEM((1, D_RNN), F32)],
        compiler_params=_cparams("arbitrary"),
        name="token_mix_prep",
    )(x2, x2, x2, g, w_cat, conv_w, conv_b, shifts, wa, ba, wx, bx, lam, pos_row, freq, gq, gkv, qg, kg,
      wq_t, wk_t, wv_t)


def _attn_kernel(fixed_ref, *refs):
    if fixed_ref:
        q_ref, k_ref, v_ref, qn_ref, kn_ref, o_ref, p_scr = refs
    else:
        q_ref, k_ref, v_ref, o_ref, s_scr = refs
    tq, tk = ATT_TQ, TOK_TILE
    nsub = tq // tk
    qi = pl.program_id(1)

    def query_block(g, a):
        q = q_ref[g, a]
        if not fixed_ref:
            return q
        k_max = jnp.max(kn_ref[g], axis=0)[:, :1]
        ref = qn_ref[g, a] * k_max
        row = lax.broadcasted_iota(jnp.int32, (QK_PAD - QK_HEAD, tk), 0)
        extra = jnp.where(row == 0, -ref, 0.0).astype(BF16)
        return jnp.concatenate([q[:QK_HEAD, :], extra], axis=0)

    qs = [jnp.concatenate([query_block(g, a) for a in range(nsub)], axis=1) for g in range(ATT_HEADS)]

    def finish(accs):
        for g in range(ATT_HEADS):
            o = accs[g][:V_HEAD, :] / accs[g][V_HEAD:V_HEAD + 1, :]
            for a in range(nsub):
                o_ref[g, a] = o[:, a * tk:(a + 1) * tk]

    if fixed_ref:
        chunk_gap = ((lax.broadcasted_iota(jnp.int32, (tk, tq), 0) >> CHUNK_SHIFT)
                     - (lax.broadcasted_iota(jnp.int32, (tk, tq), 1) >> CHUNK_SHIFT))

        def produce(kb, slot):
            lead = qi * (tq >> CHUNK_SHIFT) - kb * (tk >> CHUNK_SHIFT)
            for g in range(ATT_HEADS):
                s = jnp.dot(k_ref[g, kb], qs[g], preferred_element_type=F32)
                p_scr[slot, g] = jnp.exp2(jnp.where(chunk_gap <= lead, s, NEG_BIG)).astype(BF16)

        def accumulate(kb, slot, accs):
            return tuple(accs[g] + jnp.dot(v_ref[g, kb], p_scr[slot, g], preferred_element_type=F32)
                         for g in range(ATT_HEADS))

        def pair(i, accs):
            produce(2 * i + 1, 1)
            accs = accumulate(2 * i, 0, accs)
            produce(2 * i + 2, 0)
            return accumulate(2 * i + 1, 1, accs)

        assert nsub % 2 == 0
        n_pairs = (qi + 1) * (nsub // 2)
        accs = tuple(jnp.zeros((V_AUG, tq), F32) for _ in range(ATT_HEADS))
        produce(0, 0)
        accs = lax.fori_loop(0, n_pairs - 1, pair, accs)
        kb = 2 * (n_pairs - 1)
        produce(kb + 1, 1)
        finish(accumulate(kb + 1, 1, accumulate(kb, 0, accs)))
        return

    def prefetch(kb, slot):
        for g in range(ATT_HEADS):
            s_scr[slot, g] = jnp.dot(k_ref[g, kb], qs[g], preferred_element_type=F32)

    def consume(kb, slot, state, tail):
        out = []
        for g in range(ATT_HEADS):
            m, acc = state[g]
            if tail is None:
                visible = None
            else:
                kk = (lax.broadcasted_iota(jnp.int32, (tk, tq), 0) + tail * tk) >> CHUNK_SHIFT
                qq = lax.broadcasted_iota(jnp.int32, (tk, tq), 1) >> CHUNK_SHIFT
                visible = kk <= qq
            load = lambda: s_scr[slot, g] if visible is None else jnp.where(visible, s_scr[slot, g], NEG_BIG)
            m_new = jnp.maximum(m, jnp.max(load(), axis=0, keepdims=True))
            p = jnp.exp2(load() - m_new).astype(BF16)
            acc = acc * jnp.exp2(m - m_new) + jnp.dot(v_ref[g, kb], p, preferred_element_type=F32)
            out.append((m_new, acc))
        return tuple(out)

    def pair(i, state):
        for t in range(2):
            prefetch(2 * i + t + 1, (t + 1) % 2)
            state = consume(2 * i + t, t, state, None)
        return state

    assert nsub % 2 == 0
    state = tuple((jnp.full((1, tq), NEG_BIG, F32), jnp.zeros((V_AUG, tq), F32)) for _ in range(ATT_HEADS))
    prefetch(0, 0)
    state = lax.fori_loop(0, qi * (nsub // 2), pair, state)
    kb = qi * nsub
    for t in range(nsub):
        if t + 1 < nsub:
            prefetch(kb + t + 1, (t + 1) % 2)
        state = consume(kb + t, t % 2, state, t)
    finish(tuple(acc for _, acc in state))


def _attention(q_t, k, v_t, q_norm, k_norm, fixed_ref):
    heads, nt = q_t.shape[0], q_t.shape[1]
    s = nt * TOK_TILE
    assert s % ATT_TQ == 0 and heads % ATT_HEADS == 0
    qsub = ATT_TQ // TOK_TILE
    resident = dict(pipeline_mode=pl.Buffered(1))
    in_specs = [pl.BlockSpec((ATT_HEADS, qsub, QK_PAD, TOK_TILE), lambda h, i: (h, i, 0, 0)),
                pl.BlockSpec((ATT_HEADS, nt, TOK_TILE, QK_PAD), lambda h, i: (h, 0, 0, 0), **resident),
                pl.BlockSpec((ATT_HEADS, nt, V_AUG, TOK_TILE), lambda h, i: (h, 0, 0, 0), **resident)]
    args = (q_t, k, v_t)
    scratch = [pltpu.VMEM((2, ATT_HEADS, TOK_TILE, ATT_TQ), F32)]
    if fixed_ref:
        in_specs += [pl.BlockSpec((ATT_HEADS, qsub, 1, TOK_TILE), lambda h, i: (h, i, 0, 0)),
                     pl.BlockSpec((ATT_HEADS, nt, 1, LANES), lambda h, i: (h, 0, 0, 0))]
        args += (q_norm, k_norm)
        scratch = [pltpu.VMEM((2, ATT_HEADS, TOK_TILE, ATT_TQ), BF16)]
    return pl.pallas_call(
        functools.partial(_attn_kernel, fixed_ref),
        grid=(heads // ATT_HEADS, s // ATT_TQ),
        in_specs=in_specs,
        out_specs=pl.BlockSpec((ATT_HEADS, qsub, V_HEAD, TOK_TILE), lambda h, i: (h, i, 0, 0)),
        out_shape=jax.ShapeDtypeStruct((heads, nt, V_HEAD, TOK_TILE), F32),
        scratch_shapes=scratch,
        compiler_params=_cparams("arbitrary", "arbitrary"),
        name="attention_fixed_ref" if fixed_ref else "attention_online",
    )(*args)


def _first_lane(cond, lane):
    return jnp.min(jnp.where(cond, lane.astype(F32), 4.0 * LANES), axis=1, keepdims=True).astype(jnp.int32)


def _outproj_router_kernel(x_ref, ya_ref, gb_ref, yb_ref, wo_ref, gf_ref, wrh_ref, wrl_ref, br_ref,
                           h_ref, xn_ref, ri_ref, rf_ref, cnt_ref, base):
    t = TOK_TILE

    @pl.when(pl.program_id(0) == 0)
    def _():
        base[...] = jnp.zeros((1, LANES), F32)

    yb = yb_ref[...].reshape(HEADS * V_HEAD, t).T
    merged = ya_ref[...].astype(F32) + _sigmoid(gb_ref[...].astype(F32)) * yb
    h = x_ref[...] + jnp.dot(merged.astype(BF16), wo_ref[...], preferred_element_type=F32)
    h_ref[...] = h
    xn = h * lax.rsqrt(jnp.mean(h * h, axis=-1, keepdims=True) + EPS) * gf_ref[...]
    _to_row_tiles(xn_ref, xn)

    x_hi = xn.astype(BF16)
    x_lo = (xn - x_hi.astype(F32)).astype(BF16)
    logits = (jnp.dot(x_hi, wrh_ref[...], preferred_element_type=F32)
              + jnp.dot(x_hi, wrl_ref[...], preferred_element_type=F32)
              + jnp.dot(x_lo, wrh_ref[...], preferred_element_type=F32)) + br_ref[...]

    lane = lax.broadcasted_iota(jnp.int32, (t, LANES), 1)
    is_grp = jnp.logical_and(lane >= N_EXPERTS, lane < N_EXPERTS + N_GROUPS)
    gl = jnp.where(is_grp, logits, NEG_BIG)
    gmax = jnp.max(gl, axis=1, keepdims=True)
    grp = _first_lane(gl == gmax, lane) - N_EXPERTS
    p_grp = 1.0 / jnp.sum(jnp.where(is_grp, jnp.exp(gl - gmax), 0.0), axis=1, keepdims=True)
    el = jnp.where((lane >> GROUP_SHIFT) == grp, logits, NEG_BIG)
    v1 = jnp.max(el, axis=1, keepdims=True)
    e1 = _first_lane(el == v1, lane)
    el2 = jnp.where(lane == e1, NEG_BIG, el)
    v2 = jnp.max(el2, axis=1, keepdims=True)
    e2 = _first_lane(el2 == v2, lane)
    ex = jnp.exp(v2 - v1)
    w1 = p_grp / (1.0 + ex)
    w2 = p_grp * ex / (1.0 + ex)

    tri = (lax.broadcasted_iota(jnp.int32, (t, t), 1) < lax.broadcasted_iota(jnp.int32, (t, t), 0))
    tri = jnp.where(tri, 1.0, 0.0).astype(BF16)
    oh1 = jnp.where(lane == e1, 1.0, 0.0)
    oh2 = jnp.where(lane == e2, 1.0, 0.0)
    pre1 = jnp.dot(tri, oh1.astype(BF16), preferred_element_type=F32) + base[...]
    base1 = base[...] + jnp.sum(oh1, axis=0, keepdims=True)
    pre2 = jnp.dot(tri, oh2.astype(BF16), preferred_element_type=F32) + base1
    base2 = base1 + jnp.sum(oh2, axis=0, keepdims=True)
    r1 = jnp.sum(oh1 * pre1, axis=1, keepdims=True).astype(jnp.int32)
    r2 = jnp.sum(oh2 * pre2, axis=1, keepdims=True).astype(jnp.int32)
    base[...] = base2
    cnt_ref[...] = base2

    ri = jnp.where(lane == 0, e1, jnp.where(lane == 1, e2, jnp.where(lane == 2, r1, r2)))
    ri_ref[...] = ri.astype(F32).T[:8, :].astype(jnp.int32)
    rf_ref[...] = jnp.where(lane == 0, w1, w2)


def _outproj_router(x2, ya_g, gb, yb_t, w_out, gf, wr_hi, wr_lo, br):
    s = x2.shape[0]
    t = TOK_TILE
    row = lambda n: pl.BlockSpec((t, n), lambda i: (i, 0))
    return pl.pallas_call(
        _outproj_router_kernel,
        grid=(s // t,),
        in_specs=[row(D_MODEL), row(D_MODEL), row(D_MODEL),
                  pl.BlockSpec((HEADS, None, V_HEAD, t), lambda i: (0, i, 0, 0)),
                  _const_spec((D_MODEL, D_MODEL)), _const_spec((1, D_MODEL)),
                  _const_spec((D_MODEL, LANES)), _const_spec((D_MODEL, LANES)), _const_spec((1, LANES))],
        out_specs=[row(D_MODEL), pl.BlockSpec((t * ROW_TILE, LANES), lambda i: (i, 0)),
                   pl.BlockSpec((None, 8, t), lambda i: (i, 0, 0)), row(LANES),
                   _const_spec((1, LANES))],
        out_shape=[jax.ShapeDtypeStruct((s, D_MODEL), F32), jax.ShapeDtypeStruct((s * ROW_TILE, LANES), F32),
                   jax.ShapeDtypeStruct((s // t, 8, t), jnp.int32), jax.ShapeDtypeStruct((s, LANES), F32),
                   jax.ShapeDtypeStruct((1, LANES), F32)],
        scratch_shapes=[pltpu.VMEM((1, LANES), F32)],
        compiler_params=_cparams("arbitrary"),
        name="outproj_router",
    )(x2, ya_g, gb, yb_t, w_out, gf, wr_hi, wr_lo, br)


def _row_copy(src_ref, src_row, dst_ref, dst_row, sem):
    src = src_ref.at[pl.ds(pl.multiple_of(src_row * ROW_TILE, ROW_TILE), ROW_TILE), :]
    dst = dst_ref.at[pl.ds(pl.multiple_of(dst_row * ROW_TILE, ROW_TILE), ROW_TILE), :]
    return pltpu.make_async_copy(src, dst, sem)


def _scatter_kernel(sl_ref, xn_ref, zeros_ref, xpad_ref, sem):
    del zeros_ref
    t = TOK_TILE

    for a in range(SCATTER_TILES):
        def issue(i, c, a=a):
            for k in range(2):
                _row_copy(xn_ref, a * t + i, xpad_ref, sl_ref[a, k, i], sem).start(priority=k)
            return c

        lax.fori_loop(0, t, issue, 0, unroll=DMA_UNROLL)

    def drain(i, c):
        for _ in range(2):
            _row_copy(xn_ref, 0, xpad_ref, 0, sem).wait()
        return c

    lax.fori_loop(0, SCATTER_TILES * t, drain, 0, unroll=DMA_UNROLL)


def _scatter_rows(slots, xn, xpad_zeros):
    s = xn.shape[0] // ROW_TILE
    t = SCATTER_TILES * TOK_TILE
    assert s % t == 0
    return pl.pallas_call(
        _scatter_kernel,
        grid=(s // t,),
        in_specs=[pl.BlockSpec((SCATTER_TILES, 8, TOK_TILE), lambda i: (i, 0, 0), memory_space=pltpu.SMEM),
                  pl.BlockSpec((t * ROW_TILE, LANES), lambda i: (i, 0)), pl.BlockSpec(memory_space=pl.ANY)],
        out_specs=pl.BlockSpec(memory_space=pl.ANY),
        out_shape=jax.ShapeDtypeStruct(xpad_zeros.shape, F32),
        scratch_shapes=[pltpu.SemaphoreType.DMA(())],
        input_output_aliases={2: 0},
        compiler_params=_cparams("arbitrary"),
        name="moe_scatter",
    )(slots, xn, xpad_zeros)


def _expert_kernel(be_ref, na_ref, x_ref, wg_ref, wu_ref, wd_ref, y_ref, wgu_bf, wd_bf):
    b = pl.program_id(0)
    active = b < na_ref[0]

    @pl.when(jnp.logical_and(active, jnp.logical_or(b == 0, be_ref[b] != be_ref[jnp.maximum(b - 1, 0)])))
    def _():
        wgu_bf[:, :D_EXPERT] = wg_ref[...].astype(BF16)
        wgu_bf[:, D_EXPERT:] = wu_ref[...].astype(BF16)
        wd_bf[...] = wd_ref[...].astype(BF16)

    @pl.when(active)
    def _():
        x = _from_row_tiles(x_ref, MOE_BLOCK).astype(BF16)
        gu = jnp.dot(x, wgu_bf[...], preferred_element_type=F32)
        g, u = gu[:, :D_EXPERT], gu[:, D_EXPERT:]
        hid = (g * _sigmoid(g) * u).astype(BF16)
        _to_row_tiles(y_ref, jnp.dot(hid, wd_bf[...], preferred_element_type=F32))

    @pl.when(jnp.logical_not(active))
    def _():
        y_ref[...] = jnp.zeros_like(y_ref)


def _experts(blk_expert, n_active, xpad, w_gate, w_up, w_down):
    blk_rows = MOE_BLOCK * ROW_TILE
    nblk = xpad.shape[0] // blk_rows
    rows = lambda b, be, na: (jnp.minimum(b, na[0] - 1), 0)
    expert = lambda b, be, na: (be[b], 0, 0)
    grid_spec = pltpu.PrefetchScalarGridSpec(
        num_scalar_prefetch=2,
        grid=(nblk,),
        in_specs=[pl.BlockSpec((blk_rows, LANES), rows),
                  pl.BlockSpec((None, D_MODEL, D_EXPERT), expert), pl.BlockSpec((None, D_MODEL, D_EXPERT), expert),
                  pl.BlockSpec((None, D_EXPERT, D_MODEL), expert)],
        out_specs=pl.BlockSpec((blk_rows, LANES), lambda b, be, na: (b, 0)),
        scratch_shapes=[pltpu.VMEM((D_MODEL, 2 * D_EXPERT), BF16), pltpu.VMEM((D_EXPERT, D_MODEL), BF16)],
    )
    return pl.pallas_call(
        _expert_kernel,
        grid_spec=grid_spec,
        out_shape=jax.ShapeDtypeStruct(xpad.shape, F32),
        compiler_params=_cparams("arbitrary"),
        name="moe_experts",
    )(blk_expert, n_active, xpad, w_gate, w_up, w_down)


def _combine_kernel(sl_ref, sln_ref, h_ref, rf_ref, ypad_ref, out_ref, ybuf, sem):
    t = TOK_TILE
    step = pl.program_id(0)
    cur = step % 2

    def fetch(idx_ref, slot):
        def issue(i, c):
            for k in range(2):
                _row_copy(ypad_ref, idx_ref[k, i], ybuf.at[slot, k], i,
                          sem.at[slot]).start(priority=k)
            return c

        lax.fori_loop(0, t, issue, 0, unroll=DMA_UNROLL)

    @pl.when(step == 0)
    def _():
        fetch(sl_ref, 0)

    @pl.when(step + 1 < pl.num_programs(0))
    def _():
        fetch(sln_ref, 1 - cur)

    def drain(i, c):
        for k in range(2):
            _row_copy(ypad_ref, 0, ybuf.at[cur, k], 0, sem.at[cur]).wait()
        return c

    lax.fori_loop(0, t, drain, 0, unroll=DMA_UNROLL)
    rf = rf_ref[...]
    out_ref[...] = (h_ref[...] + rf[:, 0:1] * _from_row_tiles(ybuf.at[cur, 0], t)
                    + rf[:, 1:2] * _from_row_tiles(ybuf.at[cur, 1], t))


def _combine(slots, h, rf, ypad):
    s = h.shape[0]
    t = TOK_TILE
    nt = s // t
    idx = lambda shift: pl.BlockSpec((None, 8, t), lambda i: (jnp.minimum(i + shift, nt - 1), 0, 0),
                                     memory_space=pltpu.SMEM)
    return pl.pallas_call(
        _combine_kernel,
        grid=(nt,),
        in_specs=[idx(0), idx(1), pl.BlockSpec((t, D_MODEL), lambda i: (i, 0)),
                  pl.BlockSpec((t, LANES), lambda i: (i, 0)), pl.BlockSpec(memory_space=pl.ANY)],
        out_specs=pl.BlockSpec((t, D_MODEL), lambda i: (i, 0)),
        out_shape=jax.ShapeDtypeStruct((s, D_MODEL), F32),
        scratch_shapes=[pltpu.VMEM((2, 2, t * ROW_TILE, LANES), F32), pltpu.SemaphoreType.DMA((2,))],
        compiler_params=_cparams("arbitrary"),
        name="moe_combine",
    )(slots, slots, h, rf, ypad)


def _layer(h3, positions, norm_mix_g, w_in, conv_w, conv_b, lru_wa, lru_ba, lru_wx, lru_bx, lru_lambda,
           q_a_g, w_uq, kv_a_g, w_ukv, q_norm_g, k_norm_g, w_out, norm_ffn_g,
           router_group_w, router_group_b, router_expert_w, router_expert_b, w_gate, w_up, w_down):
    b, s, _ = h3.shape
    assert b == 1 and s % TOK_TILE == 0
    x2 = h3.reshape(s, D_MODEL)
    nt = s // TOK_TILE

    o = np.cumsum((0, D_RNN, D_RNN, Q_LORA, KV_LORA, QK_ROPE))
    w_cat = jnp.concatenate(
        [w_in[:, o[0]:o[4]], w_in[:, o[4]:o[5]], jnp.zeros((D_MODEL, LANES - QK_ROPE), F32), w_in[:, o[5]:]],
        axis=1).astype(BF16)
    row = lambda v: v.reshape(1, -1)
    col = lambda v: v.reshape(-1, 1)
    wq_t = w_uq.T.astype(BF16)
    ukv = w_ukv.reshape(KV_LORA, HEADS, QK_NOPE + V_HEAD)
    wk_t = ukv[:, :, :QK_NOPE].reshape(KV_LORA, HEADS * QK_NOPE).T.astype(BF16)
    wv_t = ukv[:, :, QK_NOPE:].reshape(KV_LORA, HEADS * V_HEAD).T.astype(BF16)
    half = QK_ROPE // 2
    freq = col(ROPE_THETA ** (-jnp.arange(half, dtype=F32) / half))
    wr = jnp.concatenate([router_expert_w, router_group_w,
                          jnp.zeros((D_MODEL, LANES - N_EXPERTS - N_GROUPS), F32)], axis=1)
    wr_hi = wr.astype(BF16)
    wr_lo = (wr - wr_hi.astype(F32)).astype(BF16)
    br = row(jnp.concatenate([router_expert_b, router_group_b, jnp.zeros((LANES - N_EXPERTS - N_GROUPS,), F32)]))

    ya_g, gb, q_t, k_r, v_t, q_norm, k_norm = _token_mix_prep(
        x2, row(norm_mix_g), w_cat, conv_w, row(conv_b), lru_wa.astype(BF16), row(lru_ba), lru_wx.astype(BF16),
        row(lru_bx), row(lru_lambda), positions.reshape(nt, 1, TOK_TILE), freq, col(q_a_g), col(kv_a_g),
        col(q_norm_g), col(k_norm_g), wq_t, wk_t, wv_t)
    bound = jnp.max(jnp.max(q_norm, axis=(1, 2, 3)) * jnp.max(k_norm, axis=(1, 2, 3)))
    yb_t = lax.cond(bound <= ATT_FIXED_REF_LIMIT,
                    lambda: _attention(q_t, k_r, v_t, q_norm, k_norm, True),
                    lambda: _attention(q_t, k_r, v_t, q_norm, k_norm, False))
    h, xn, ri, rf, counts = _outproj_router(x2, ya_g, gb, yb_t, w_out.astype(BF16), row(norm_ffn_g),
                                            wr_hi, wr_lo, br)

    cnt = counts[0, :N_EXPERTS].astype(jnp.int32)
    padded = (cnt + MOE_BLOCK - 1) // MOE_BLOCK * MOE_BLOCK
    p_end = jnp.cumsum(padded)
    p_start = p_end - padded
    n_slots = 2 * s + N_EXPERTS * MOE_BLOCK
    nblk = n_slots // MOE_BLOCK
    blk_first = jnp.arange(nblk, dtype=jnp.int32) * MOE_BLOCK
    blk_expert = jnp.minimum(jnp.sum((p_end[None, :] <= blk_first[:, None]).astype(jnp.int32), axis=1),
                             N_EXPERTS - 1)
    n_active = (p_end[-1:] // MOE_BLOCK).astype(jnp.int32)
    hit = ri[:, 0:2, :, None] == jnp.arange(N_EXPERTS, dtype=jnp.int32)
    slots = ri[:, 2:4, :] + jnp.sum(jnp.where(hit, p_start.astype(jnp.int32), 0), axis=-1)
    slots = jnp.pad(slots, ((0, 0), (0, 6), (0, 0)))

    xpad = _scatter_rows(slots, xn, jnp.zeros((n_slots * ROW_TILE, LANES), F32))
    ypad = _experts(blk_expert, n_active, xpad, w_gate, w_up, w_down)
    out = _combine(slots, h, rf, ypad)
    return out.reshape(b, s, D_MODEL)


def kernel(x, positions, norm_mix_g, w_in, conv_w, conv_b, lru_wa, lru_ba, lru_wx, lru_bx, lru_lambda, q_a_g,
           w_uq, kv_a_g, w_ukv, q_norm_g, k_norm_g, w_out, norm_ffn_g, router_group_w, router_group_b,
           router_expert_w, router_expert_b, w_gate, w_up, w_down):
    h = x
    for l in range(norm_mix_g.shape[0]):
        h = _layer(h, positions, norm_mix_g[l], w_in[l], conv_w[l], conv_b[l], lru_wa[l], lru_ba[l], lru_wx[l],
                   lru_bx[l], lru_lambda[l], q_a_g[l], w_uq[l], kv_a_g[l], w_ukv[l], q_norm_g[l], k_norm_g[l],
                   w_out[l], norm_ffn_g[l], router_group_w[l], router_group_b[l], router_expert_w[l],
                   router_expert_b[l], w_gate[l], w_up[l], w_down[l])
    return h
```
